```python
import math
import jax, jax.numpy as jnp
from jax import lax
import numpy as np

D_MODEL = 4096
BATCH = 4
SEQ = 2048
DEPTH = 2
DEC_BATCH = 128
DEC_SEQ = 1
PAST_LEN = 16384
PAGE_SIZE = 128

ML_HEADS = 8
ML_DQK = 128
ML_DV = 256
ML_CHUNK = 64
GATE_CAP = 15.0
MLA_HEADS = 16
Q_LORA = 1024
KV_LORA = 512
NOPE_DIM = 128
ROPE_DIM = 64
V_DIM = 128
ROPE_THETA = 10000.0
Q_BLOCK = 128
N_MEM = 256
MEM_HEADS = 4
MEM_DIM = 128
D_FF = 5504
EPS = 1e-6

ML_QK_W = ML_HEADS * ML_DQK
ML_V_W = ML_HEADS * ML_DV
MLA_V_W = MLA_HEADS * V_DIM
MEM_W = MEM_HEADS * MEM_DIM
SPLITS = (ML_QK_W, ML_QK_W, ML_V_W, ML_HEADS, ML_HEADS, ML_V_W, Q_LORA, KV_LORA, ROPE_DIM, D_MODEL, D_MODEL)
IN_COLS = 2 * ML_QK_W + 2 * ML_V_W + 2 * ML_HEADS + Q_LORA + KV_LORA + ROPE_DIM + 2 * D_MODEL

kernel_name = "hybrid_mlstm_mla_macaron_step"


def rmsnorm(x, g):
    x32 = x.astype(jnp.float32)
    y = x32 * lax.rsqrt(jnp.mean(x32 * x32, axis=-1, keepdims=True) + EPS)
    return (y * g.astype(jnp.float32)).astype(x.dtype)


def swiglu(h, w_gate, w_up, w_down):
    return (jax.nn.silu(h @ w_gate) * (h @ w_up)) @ w_down


def softcap(a):
    return GATE_CAP * jnp.tanh(a / GATE_CAP)


def rope(x, pos):
    half = ROPE_DIM // 2
    inv = ROPE_THETA ** (-jnp.arange(half, dtype=jnp.float32) / half)
    ang = pos.astype(jnp.float32)[:, None] * inv[None, :]
    shape = (ang.shape[0],) + (1,) * (x.ndim - 3) + (half,)
    cos = jnp.cos(ang).reshape(shape)
    sin = jnp.sin(ang).reshape(shape)
    x32 = x.astype(jnp.float32)
    x1, x2 = x32[..., :half], x32[..., half:]
    return jnp.concatenate([x1 * cos - x2 * sin, x2 * cos + x1 * sin], axis=-1).astype(x.dtype)


def split_in(z):
    idx = []
    acc = 0
    for s in SPLITS[:-1]:
        acc += s
        idx.append(acc)
    return jnp.split(z, idx, axis=-1)


def mlstm_chunk(q, k, v, ig, fg, C, n, m):
    L = q.shape[2]
    b = jnp.cumsum(fg, axis=-1)
    causal = jnp.tril(jnp.ones((L, L), dtype=bool))
    logd = jnp.where(causal, b[..., :, None] - b[..., None, :] + ig[..., None, :], -jnp.inf)
    loga = b + m[..., None]
    m_t = jnp.maximum(loga, jnp.max(logd, axis=-1))
    dmat = jnp.exp(logd - m_t[..., None])
    a = jnp.exp(loga - m_t)
    s = jnp.einsum('bhtd,bhsd->bhts', q, k) * dmat
    num = a[..., None] * jnp.einsum('bhtd,bhde->bhte', q, C) + jnp.einsum('bhts,bhse->bhte', s, v)
    den = a * jnp.einsum('bhtd,bhd->bht', q, n) + jnp.sum(s, axis=-1)
    h = num / jnp.maximum(jnp.abs(den), jnp.exp(-m_t))[..., None]
    m_new = m_t[..., -1]
    w = jnp.exp(b[..., -1:] - b + ig - m_new[..., None])
    a_end = a[..., -1]
    C_new = a_end[..., None, None] * C + jnp.einsum('bhs,bhsd,bhse->bhde', w, k, v)
    n_new = a_end[..., None] * n + jnp.einsum('bhs,bhsd->bhd', w, k)
    return h, C_new, n_new, m_new


def mlstm_prompt(q, k, v, ig, fg):
    B, S = q.shape[0], q.shape[1]
    nc = S // ML_CHUNK
    def chunks(t):
        t = t.reshape((B, nc, ML_CHUNK) + t.shape[2:])
        return jnp.moveaxis(jnp.swapaxes(t, 2, 3), 1, 0)
    init = (jnp.zeros((B, ML_HEADS, ML_DQK, ML_DV), jnp.float32),
            jnp.zeros((B, ML_HEADS, ML_DQK), jnp.float32),
            jnp.zeros((B, ML_HEADS), jnp.float32))
    def step(carry, xs):
        h, C, n, m = mlstm_chunk(*xs, *carry)
        return (C, n, m), h
    (C, n, m), hs = lax.scan(step, init, (chunks(q), chunks(k), chunks(v), chunks(ig), chunks(fg)))
    h = hs.transpose(1, 0, 3, 2, 4).reshape(B, S, ML_HEADS, ML_DV)
    return h, (C, n, m)


def mlstm_sample(q, k, v, ig, fg, C, n, m):
    h, C, n, m = mlstm_chunk(q.transpose(0, 2, 1, 3), k.transpose(0, 2, 1, 3), v.transpose(0, 2, 1, 3),
                             ig.transpose(0, 2, 1), fg.transpose(0, 2, 1),
                             C.astype(jnp.float32), n.astype(jnp.float32), m.astype(jnp.float32))
    return h.transpose(0, 2, 1, 3), (C, n, m)


def mla_prompt(q_lat, q_rope, c_kv, k_rope):
    B, S, H, _ = q_lat.shape
    nb = S // Q_BLOCK
    scale = (NOPE_DIM + ROPE_DIM) ** -0.5
    kpos = jnp.arange(S)
    qpos = kpos.reshape(nb, Q_BLOCK)
    ql = q_lat.reshape(B, nb, Q_BLOCK, H, KV_LORA).transpose(1, 0, 2, 3, 4)
    qr = q_rope.reshape(B, nb, Q_BLOCK, H, ROPE_DIM).transpose(1, 0, 2, 3, 4)
    def block(args):
        qlb, qrb, pos = args
        s = (jnp.einsum('bqhc,bkc->bhqk', qlb, c_kv)
             + jnp.einsum('bqhr,bkr->bhqk', qrb, k_rope)).astype(jnp.float32) * scale
        s = jnp.where(kpos[None, :] <= pos[:, None], s, -jnp.inf)
        p = jax.nn.softmax(s, axis=-1)
        return jnp.einsum('bhqk,bkc->bqhc', p, c_kv.astype(jnp.float32))
    out = lax.map(block, (ql, qr, qpos))
    return out.transpose(1, 0, 2, 3, 4).reshape(B, S, H, KV_LORA)


def mla_sample(q_lat, q_rope, c_new, kr_new, cache_lat, cache_kr, layer, page_table):
    T = q_lat.shape[1]
    scale = (NOPE_DIM + ROPE_DIM) ** -0.5
    def scores(c, kr):
        return (jnp.einsum('bthc,bkc->bhtk', q_lat, c)
                + jnp.einsum('bthr,bkr->bhtk', q_rope, kr)).astype(jnp.float32) * scale
    s0 = jnp.where(jnp.tril(jnp.ones((T, T), dtype=bool)), scores(c_new, kr_new), -jnp.inf)
    m0 = jnp.max(s0, axis=-1)
    p0 = jnp.exp(s0 - m0[..., None])
    init = (m0, jnp.sum(p0, axis=-1), jnp.einsum('bhtk,bkc->bhtc', p0, c_new.astype(jnp.float32)))
    def step(carry, phys):
        m, l, acc = carry
        c = cache_lat[layer, phys]
        kr = cache_kr[layer, phys]
        s = scores(c, kr)
        m_new = jnp.maximum(m, jnp.max(s, axis=-1))
        alpha = jnp.exp(m - m_new)
        p = jnp.exp(s - m_new[..., None])
        acc = acc * alpha[..., None] + jnp.einsum('bhtk,bkc->bhtc', p, c.astype(jnp.float32))
        return (m_new, l * alpha + jnp.sum(p, axis=-1), acc), None
    (m, l, acc), _ = lax.scan(step, init, page_table.T)
    return (acc / l[..., None]).transpose(0, 2, 1, 3)


def mem_kv(mem, g, w_k, w_v):
    B, M, _ = mem.shape
    hm = rmsnorm(mem, g)
    return ((hm @ w_k).reshape(B, M, MEM_HEADS, MEM_DIM), (hm @ w_v).reshape(B, M, MEM_HEADS, MEM_DIM))


def mem_attend(h, w_q, w_o, mem_k, mem_v):
    B, S, _ = h.shape
    q = (h @ w_q).reshape(B, S, MEM_HEADS, MEM_DIM)
    s = jnp.einsum('bshd,bmhd->bhsm', q, mem_k).astype(jnp.float32) * (MEM_DIM ** -0.5)
    p = jax.nn.softmax(s, axis=-1)
    o = jnp.einsum('bhsm,bmhd->bshd', p, mem_v.astype(jnp.float32))
    return o.reshape(B, S, MEM_W).astype(h.dtype) @ w_o


def layer_step(x, pos, lw, mem_k, mem_v, past):
    B, S, _ = x.shape
    f32 = jnp.float32
    x = x + 0.5 * swiglu(rmsnorm(x, lw['norm_ffn1']), lw['ffn1_w_gate'], lw['ffn1_w_up'], lw['ffn1_w_down'])
    h = rmsnorm(x, lw['norm_mix'])
    q_m, k_m, v_m, i_m, f_m, o_m, cq, ckv, kr, g_a, g_b = split_in(h @ lw['w_in'])
    q_m = q_m.reshape(B, S, ML_HEADS, ML_DQK).astype(f32)
    k_m = k_m.reshape(B, S, ML_HEADS, ML_DQK).astype(f32) * (ML_DQK ** -0.5)
    v_m = v_m.reshape(B, S, ML_HEADS, ML_DV).astype(f32)
    ig = softcap(i_m.astype(f32) + lw['mlstm_b_i'].astype(f32))
    fg = jax.nn.log_sigmoid(softcap(f_m.astype(f32) + lw['mlstm_b_f'].astype(f32)))
    if past is None:
        h_m, (C, n, m) = mlstm_prompt(q_m, k_m, v_m, ig, fg)
    else:
        h_m, (C, n, m) = mlstm_sample(q_m, k_m, v_m, ig, fg, past[0], past[1], past[2])
    h_m = rmsnorm(h_m, lw['mlstm_norm']).reshape(B, S, ML_V_W) * jax.nn.sigmoid(o_m.astype(f32))
    q = (rmsnorm(cq, lw['mla_norm_q']) @ lw['mla_w_uq']).reshape(B, S, MLA_HEADS, NOPE_DIM + ROPE_DIM)
    q_rope = rope(q[..., NOPE_DIM:], pos)
    q_lat = jnp.einsum('bshd,chd->bshc', q[..., :NOPE_DIM], lw['mla_w_uk'])
    c_kv = rmsnorm(ckv, lw['mla_norm_kv'])
    k_rope = rope(kr, pos)
    if past is None:
        o_lat = mla_prompt(q_lat, q_rope, c_kv, k_rope)
    else:
        o_lat = mla_sample(q_lat, q_rope, c_kv, k_rope, past[3], past[4], past[5], past[6])
    h_a = jnp.einsum('bshc,chd->bshd', o_lat, lw['mla_w_uv']).reshape(B, S, MLA_V_W)
    merged = (jax.nn.sigmoid(g_a) * (h_m.astype(x.dtype) @ lw['w_branch_mlstm'])
              + jax.nn.sigmoid(g_b) * (h_a.astype(x.dtype) @ lw['w_branch_mla']))
    x = x + merged @ lw['w_out']
    x = x + mem_attend(rmsnorm(x, lw['norm_mem']), lw['mem_w_q'], lw['mem_w_o'], mem_k, mem_v)
    x = x + 0.5 * swiglu(rmsnorm(x, lw['norm_ffn2']), lw['ffn2_w_gate'], lw['ffn2_w_up'], lw['ffn2_w_down'])
    return x, (C, n, m, c_kv, k_rope)


def setup_inputs(seed: int = 0) -> dict:
    key = jax.random.key(seed)
    ks = iter(jax.random.split(key, 48))
    f32 = jnp.float32
    def nrm(shape, scale):
        return jax.random.normal(next(ks), shape, f32) * scale
    def gain(shape):
        return 1.0 + 0.02 * jax.random.normal(next(ks), shape, f32)
    n_pages = PAST_LEN // PAGE_SIZE
    n_pool = (DEC_BATCH * n_pages * 5) // 4
    perm = jax.random.permutation(next(ks), n_pool)[:DEC_BATCH * n_pages]
    page_table = perm.reshape(DEC_BATCH, n_pages).astype(jnp.int32)
    D = D_MODEL
    return {
        'x_prompt': nrm((BATCH, SEQ, D), 1.0),
        'mem_prompt': nrm((BATCH, N_MEM, D), 1.0),
        'x_sample': nrm((DEC_BATCH, DEC_SEQ, D), 1.0),
        'state_mlstm_C': nrm((DEPTH, DEC_BATCH, ML_HEADS, ML_DQK, ML_DV), 0.5),
        'state_mlstm_n': nrm((DEPTH, DEC_BATCH, ML_HEADS, ML_DQK), 0.5),
        'state_mlstm_m': nrm((DEPTH, DEC_BATCH, ML_HEADS), 1.0),
        'cache_mla_latent': nrm((DEPTH, n_pool, PAGE_SIZE, KV_LORA), 1.0),
        'cache_mla_krope': nrm((DEPTH, n_pool, PAGE_SIZE, ROPE_DIM), 1.0),
        'cache_mem_k': nrm((DEPTH, DEC_BATCH, N_MEM, MEM_HEADS, MEM_DIM), 1.0),
        'cache_mem_v': nrm((DEPTH, DEC_BATCH, N_MEM, MEM_HEADS, MEM_DIM), 1.0),
        'page_table': page_table,
        'norm_ffn1': gain((DEPTH, D)),
        'ffn1_w_gate': nrm((DEPTH, D, D_FF), D ** -0.5),
        'ffn1_w_up': nrm((DEPTH, D, D_FF), D ** -0.5),
        'ffn1_w_down': nrm((DEPTH, D_FF, D), D_FF ** -0.5),
        'norm_mix': gain((DEPTH, D)),
        'w_in': nrm((DEPTH, D, IN_COLS), D ** -0.5),
        'mlstm_b_i': nrm((DEPTH, ML_HEADS), 0.1),
        'mlstm_b_f': 3.0 + nrm((DEPTH, ML_HEADS), 0.5),
        'mlstm_norm': gain((DEPTH, ML_HEADS, ML_DV)),
        'mla_norm_q': gain((DEPTH, Q_LORA)),
        'mla_w_uq': nrm((DEPTH, Q_LORA, MLA_HEADS * (NOPE_DIM + ROPE_DIM)), Q_LORA ** -0.5),
        'mla_norm_kv': gain((DEPTH, KV_LORA)),
        'mla_w_uk': nrm((DEPTH, KV_LORA, MLA_HEADS, NOPE_DIM), KV_LORA ** -0.5),
        'mla_w_uv': nrm((DEPTH, KV_LORA, MLA_HEADS, V_DIM), KV_LORA ** -0.5),
        'w_branch_mlstm': nrm((DEPTH, ML_V_W, D), ML_V_W ** -0.5),
        'w_branch_mla': nrm((DEPTH, MLA_V_W, D), MLA_V_W ** -0.5),
        'w_out': nrm((DEPTH, D, D), D ** -0.5),
        'norm_mem': gain((DEPTH, D)),
        'norm_mem_src': gain((DEPTH, D)),
        'mem_w_q': nrm((DEPTH, D, MEM_W), D ** -0.5),
        'mem_w_k': nrm((DEPTH, D, MEM_W), D ** -0.5),
        'mem_w_v': nrm((DEPTH, D, MEM_W), D ** -0.5),
        'mem_w_o': nrm((DEPTH, MEM_W, D), MEM_W ** -0.5),
        'norm_ffn2': gain((DEPTH, D)),
        'ffn2_w_gate': nrm((DEPTH, D, D_FF), D ** -0.5),
        'ffn2_w_up': nrm((DEPTH, D, D_FF), D ** -0.5),
        'ffn2_w_down': nrm((DEPTH, D_FF, D), D_FF ** -0.5),
        'norm_final': gain((D,)),
    }


def reference(x_prompt, mem_prompt, x_sample, state_mlstm_C, state_mlstm_n, state_mlstm_m,
              cache_mla_latent, cache_mla_krope, cache_mem_k, cache_mem_v, page_table,
              norm_ffn1, ffn1_w_gate, ffn1_w_up, ffn1_w_down, norm_mix, w_in, mlstm_b_i, mlstm_b_f,
              mlstm_norm, mla_norm_q, mla_w_uq, mla_norm_kv, mla_w_uk, mla_w_uv, w_branch_mlstm,
              w_branch_mla, w_out, norm_mem, norm_mem_src, mem_w_q, mem_w_k, mem_w_v, mem_w_o,
              norm_ffn2, ffn2_w_gate, ffn2_w_up, ffn2_w_down, norm_final):
    pos_p = jnp.arange(x_prompt.shape[1])
    past_len = page_table.shape[1] * cache_mla_latent.shape[2]
    pos_s = past_len + jnp.arange(x_sample.shape[1])
    xp, xs = x_prompt, x_sample
    out_p = [[] for _ in range(7)]
    out_s = [[] for _ in range(5)]
    for l in range(DEPTH):
        lw = dict(norm_ffn1=norm_ffn1[l], ffn1_w_gate=ffn1_w_gate[l], ffn1_w_up=ffn1_w_up[l],
                  ffn1_w_down=ffn1_w_down[l], norm_mix=norm_mix[l], w_in=w_in[l],
                  mlstm_b_i=mlstm_b_i[l], mlstm_b_f=mlstm_b_f[l], mlstm_norm=mlstm_norm[l],
                  mla_norm_q=mla_norm_q[l], mla_w_uq=mla_w_uq[l], mla_norm_kv=mla_norm_kv[l],
                  mla_w_uk=mla_w_uk[l], mla_w_uv=mla_w_uv[l], w_branch_mlstm=w_branch_mlstm[l],
                  w_branch_mla=w_branch_mla[l], w_out=w_out[l], norm_mem=norm_mem[l],
                  mem_w_q=mem_w_q[l], mem_w_o=mem_w_o[l], norm_ffn2=norm_ffn2[l],
                  ffn2_w_gate=ffn2_w_gate[l], ffn2_w_up=ffn2_w_up[l], ffn2_w_down=ffn2_w_down[l])
        mk, mv = mem_kv(mem_prompt, norm_mem_src[l], mem_w_k[l], mem_w_v[l])
        xp, st_p = layer_step(xp, pos_p, lw, mk, mv, None)
        past = (state_mlstm_C[l], state_mlstm_n[l], state_mlstm_m[l], cache_mla_latent, cache_mla_krope, l, page_table)
        xs, st_s = layer_step(xs, pos_s, lw, cache_mem_k[l], cache_mem_v[l], past)
        for lst, a in zip(out_p, st_p + (mk, mv)):
            lst.append(a)
        for lst, a in zip(out_s, st_s):
            lst.append(a)
    y_prompt = rmsnorm(xp, norm_final)
    y_sample = rmsnorm(xs, norm_final)
    p_C, p_n, p_m, p_lat, p_kr, p_mk, p_mv = [jnp.stack(a) for a in out_p]
    s_C, s_n, s_m, s_lat, s_kr = [jnp.stack(a) for a in out_s]
    return (y_prompt, y_sample, p_C, p_n, p_m, p_lat, p_kr, p_mk, p_mv, s_C, s_n, s_m, s_lat, s_kr)
```

```python
import functools
import math

import jax
import jax.numpy as jnp
from jax import lax
from jax.experimental import pallas as pl
from jax.experimental.pallas import tpu as pltpu

F32 = jnp.float32
BF16 = jnp.bfloat16

EPS = 1e-6
GATE_CAP = 15.0
ROPE_THETA = 10000.0

LANES = 128
VMEM_BYTES = 64 * 1024 * 1024
VMEM_TEMP_BYTES = 12 * 1024 * 1024


def _nbytes(shape, dtype):
    return math.prod(shape) * jnp.dtype(dtype).itemsize


def _params(semantics, blocks, scratch_bytes=0):
    need = 2 * sum(_nbytes(s, d) for s, d in blocks) + scratch_bytes + VMEM_TEMP_BYTES
    return pltpu.CompilerParams(dimension_semantics=semantics,
                                vmem_limit_bytes=min(need, VMEM_BYTES - 4 * 1024 * 1024))


def _dot(a, b):
    return jnp.dot(a, b, preferred_element_type=F32)


def _dot_nt(a, b):
    return lax.dot_general(a, b, (((1,), (1,)), ((), ())), preferred_element_type=F32)


def _dot_tn(a, b):
    return lax.dot_general(a, b, (((0,), (0,)), ((), ())), preferred_element_type=F32)


def _softcap(a):
    return GATE_CAP * jnp.tanh(a / GATE_CAP)


def _log_sigmoid(a):
    return jnp.minimum(a, 0.0) - jnp.log1p(jnp.exp(-jnp.abs(a)))


def _rms(x, g):
    return x * lax.rsqrt(jnp.mean(x * x, axis=-1, keepdims=True) + EPS) * g


def _rope_rotate(x):
    lane = lax.broadcasted_iota(jnp.int32, x.shape, x.ndim - 1)
    first_half = (lane % 64) < 32
    return jnp.where(first_half, -pltpu.roll(x, 96, x.ndim - 1), pltpu.roll(x, 32, x.ndim - 1))


def _rmsnorm_body(x_ref, g_ref, o_ref):
    o_ref[...] = _rms(x_ref[...].astype(F32), g_ref[...].astype(F32)).astype(o_ref.dtype)


def rmsnorm(x, g, out_dtype, *, tm):
    m, w = x.shape
    blocks = [((tm, w), x.dtype), ((tm, w), out_dtype)]
    return pl.pallas_call(
        _rmsnorm_body,
        grid=(m // tm,),
        in_specs=[pl.BlockSpec((tm, w), lambda i: (i, 0)),
                  pl.BlockSpec((1, w), lambda i: (0, 0))],
        out_specs=pl.BlockSpec((tm, w), lambda i: (i, 0)),
        out_shape=jax.ShapeDtypeStruct((m, w), out_dtype),
        compiler_params=_params(("parallel",), blocks),
        name="rmsnorm",
    )(x, g.reshape(1, w))


def _mm_body(a_ref, w_ref, o_ref):
    o_ref[...] = _dot(a_ref[...], w_ref[...]).astype(o_ref.dtype)


def matmul(a, w, out_dtype, *, tm, tn):
    m, k = a.shape
    n = w.shape[1]
    blocks = [((tm, k), a.dtype), ((k, tn), w.dtype), ((tm, tn), out_dtype)]
    return pl.pallas_call(
        _mm_body,
        grid=(m // tm, pl.cdiv(n, tn)),
        in_specs=[pl.BlockSpec((tm, k), lambda i, j: (i, 0)),
                  pl.BlockSpec((k, tn), lambda i, j: (0, j))],
        out_specs=pl.BlockSpec((tm, tn), lambda i, j: (i, j)),
        out_shape=jax.ShapeDtypeStruct((m, n), out_dtype),
        compiler_params=_params(("parallel", "arbitrary"), blocks, _nbytes((tm, tn), F32)),
        name="matmul",
    )(a, w)


def _mm_residual_body(a_ref, w_ref, r_ref, o_ref, *, scale):
    o_ref[...] = r_ref[...] + scale * _dot(a_ref[...], w_ref[...])


def matmul_residual(a, w, res, scale, *, tm, tn):
    m, k = a.shape
    n = w.shape[1]
    blocks = [((tm, k), a.dtype), ((k, tn), w.dtype), ((tm, tn), F32), ((tm, tn), F32)]
    return pl.pallas_call(
        functools.partial(_mm_residual_body, scale=scale),
        grid=(m // tm, n // tn),
        in_specs=[pl.BlockSpec((tm, k), lambda i, j: (i, 0)),
                  pl.BlockSpec((k, tn), lambda i, j: (0, j)),
                  pl.BlockSpec((tm, tn), lambda i, j: (i, j))],
        out_specs=pl.BlockSpec((tm, tn), lambda i, j: (i, j)),
        out_shape=jax.ShapeDtypeStruct((m, n), F32),
        compiler_params=_params(("parallel", "arbitrary"), blocks, _nbytes((tm, tn), F32)),
        name="matmul_residual",
    )(a, w, res)


def _swiglu_body(a_ref, wg_ref, wu_ref, o_ref):
    a = a_ref[...]
    g = _dot(a, wg_ref[...])
    u = _dot(a, wu_ref[...])
    o_ref[...] = (g * jax.nn.sigmoid(g) * u).astype(o_ref.dtype)


def swiglu_up(a, wg, wu, *, tm, tn):
    m, k = a.shape
    n = wg.shape[1]
    blocks = [((tm, k), a.dtype), ((k, tn), wg.dtype), ((k, tn), wu.dtype), ((tm, tn), BF16)]
    return pl.pallas_call(
        _swiglu_body,
        grid=(m // tm, pl.cdiv(n, tn)),
        in_specs=[pl.BlockSpec((tm, k), lambda i, j: (i, 0)),
                  pl.BlockSpec((k, tn), lambda i, j: (0, j)),
                  pl.BlockSpec((k, tn), lambda i, j: (0, j))],
        out_specs=pl.BlockSpec((tm, tn), lambda i, j: (i, j)),
        out_shape=jax.ShapeDtypeStruct((m, n), BF16),
        compiler_params=_params(("parallel", "arbitrary"), blocks, 3 * _nbytes((tm, tn), F32)),
        name="swiglu_up",
    )(a, wg, wu)


def _merge_body(hm_ref, ha_ref, wa_ref, wb_ref, ga_ref, gb_ref, o_ref):
    ya = _dot(hm_ref[...], wa_ref[...])
    yb = _dot(ha_ref[...], wb_ref[...])
    o_ref[...] = (jax.nn.sigmoid(ga_ref[...]) * ya + jax.nn.sigmoid(gb_ref[...]) * yb).astype(o_ref.dtype)


def gated_merge(hm, ha, wa, wb, z, ga_col, gb_col, *, tm, tn):
    m, ka = hm.shape
    kb = ha.shape[1]
    n = wa.shape[1]
    ja, jb = ga_col // tn, gb_col // tn
    assert ja * tn == ga_col and jb * tn == gb_col
    blocks = [((tm, ka), BF16), ((tm, kb), BF16), ((ka, tn), BF16), ((kb, tn), BF16),
              ((tm, tn), F32), ((tm, tn), F32), ((tm, tn), BF16)]
    return pl.pallas_call(
        _merge_body,
        grid=(m // tm, n // tn),
        in_specs=[pl.BlockSpec((tm, ka), lambda i, j: (i, 0)),
                  pl.BlockSpec((tm, kb), lambda i, j: (i, 0)),
                  pl.BlockSpec((ka, tn), lambda i, j: (0, j)),
                  pl.BlockSpec((kb, tn), lambda i, j: (0, j)),
                  pl.BlockSpec((tm, tn), lambda i, j: (i, ja + j)),
                  pl.BlockSpec((tm, tn), lambda i, j: (i, jb + j))],
        out_specs=pl.BlockSpec((tm, tn), lambda i, j: (i, j)),
        out_shape=jax.ShapeDtypeStruct((m, n), BF16),
        compiler_params=_params(("parallel", "arbitrary"), blocks, 3 * _nbytes((tm, tn), F32)),
        name="gated_merge",
    )(hm, ha, wa, wb, z, z)


def _mlstm_prompt_body(bi_ref, bf_ref, q_ref, k_ref, v_ref, o_ref, zm_ref, grow_ref, gain_ref,
                       h_ref, c_ref, n_ref, m_ref, *, chunk, heads, gate_lane, k_scale):
    hd = pl.program_id(1)

    @pl.when(pl.program_id(2) == 0)
    def _():
        c_ref[...] = jnp.zeros_like(c_ref)
        n_ref[...] = jnp.zeros_like(n_ref)
        m_ref[...] = jnp.zeros_like(m_ref)

    b_i = bi_ref[hd]
    b_f = bf_ref[hd]
    zm = zm_ref[...]
    lane = lax.broadcasted_iota(jnp.int32, zm.shape, 1)
    i_col = jnp.sum(jnp.where(lane == gate_lane + hd, zm, 0.0), axis=-1, keepdims=True)
    f_col = jnp.sum(jnp.where(lane == gate_lane + heads + hd, zm, 0.0), axis=-1, keepdims=True)
    grow = grow_ref[0, 0]
    ig_col = _softcap(i_col + b_i)
    fg_col = _log_sigmoid(_softcap(f_col + b_f))
    ig_row = _softcap(grow[0:1, :] + b_i)
    fg_row = _log_sigmoid(_softcap(grow[1:2, :] + b_f))

    t_idx = lax.broadcasted_iota(jnp.int32, (chunk, chunk), 0)
    s_idx = lax.broadcasted_iota(jnp.int32, (chunk, chunk), 1)
    causal = s_idx <= t_idx
    b_col = jnp.sum(jnp.where(causal, fg_row, 0.0), axis=-1, keepdims=True)
    b_row = jnp.sum(jnp.where(t_idx <= s_idx, fg_col, 0.0), axis=0, keepdims=True)

    m_prev = m_ref[0, 0]
    logd = jnp.where(causal, b_col - b_row + ig_row, -jnp.inf)
    loga = b_col + m_prev
    m_t = jnp.maximum(loga, jnp.max(logd, axis=-1, keepdims=True))
    dmat = jnp.exp(logd - m_t)
    a = jnp.exp(loga - m_t)

    q = q_ref[...]
    k = k_ref[...] * k_scale
    v = v_ref[...]
    qb, kb, vb = q.astype(BF16), k.astype(BF16), v.astype(BF16)
    c_old = c_ref[0, 0]
    n_old = n_ref[0, 0]
    s = _dot_nt(qb, kb) * dmat
    num = a * _dot(qb, c_old.astype(BF16)) + _dot(s.astype(BF16), vb)
    den = a * jnp.sum(q * n_old, axis=-1, keepdims=True) + jnp.sum(s, axis=-1, keepdims=True)
    h = num / jnp.maximum(jnp.abs(den), jnp.exp(-m_t))
    h = _rms(h, gain_ref[0]) * jax.nn.sigmoid(o_ref[...])
    h_ref[...] = h.astype(h_ref.dtype)

    b_end = b_col[chunk - 1:chunk, :]
    m_new = m_t[chunk - 1:chunk, :]
    a_end = a[chunk - 1:chunk, :]
    w_col = jnp.exp(b_end - b_col + ig_col - m_new)
    c_ref[0, 0] = a_end * c_old + _dot_tn(kb, (w_col * v).astype(BF16))
    n_ref[0, 0] = a_end * n_old + jnp.sum(w_col * k, axis=0, keepdims=True)
    m_ref[0, 0] = m_new


def mlstm_prompt(z, gates_row, b_i, b_f, gain, lay, *, batch, seq, chunk):
    heads, dqk, dv = lay["ml_heads"], lay["ml_dqk"], lay["ml_dv"]
    nc = seq // chunk
    kq, kk = lay["q"] // dqk, lay["k"] // dqk
    kv, ko = lay["v"] // dv, lay["o"] // dv
    kmisc = lay["misc"] // LANES
    row = lambda b, h, c: b * nc + c
    blocks = [((chunk, dqk), F32)] * 2 + [((chunk, dv), F32)] * 2 + [((chunk, LANES), F32),
              ((chunk, dv), BF16), ((dqk, dv), F32)]
    smem = pl.BlockSpec(memory_space=pltpu.SMEM)
    body = functools.partial(_mlstm_prompt_body, chunk=chunk, heads=heads,
                             gate_lane=lay["misc_gate_lane"], k_scale=dqk ** -0.5)
    return pl.pallas_call(
        body,
        grid=(batch, heads, nc),
        in_specs=[smem, smem,
                  pl.BlockSpec((chunk, dqk), lambda b, h, c: (row(b, h, c), kq + h)),
                  pl.BlockSpec((chunk, dqk), lambda b, h, c: (row(b, h, c), kk + h)),
                  pl.BlockSpec((chunk, dv), lambda b, h, c: (row(b, h, c), kv + h)),
                  pl.BlockSpec((chunk, dv), lambda b, h, c: (row(b, h, c), ko + h)),
                  pl.BlockSpec((chunk, LANES), lambda b, h, c: (row(b, h, c), kmisc)),
                  pl.BlockSpec((1, 1, 2, chunk), lambda b, h, c: (b, h, 0, c)),
                  pl.BlockSpec((1, 1, dv), lambda b, h, c: (h, 0, 0))],
        out_specs=[pl.BlockSpec((chunk, dv), lambda b, h, c: (row(b, h, c), h)),
                   pl.BlockSpec((1, 1, dqk, dv), lambda b, h, c: (b, h, 0, 0)),
                   pl.BlockSpec((1, 1, 1, dqk), lambda b, h, c: (b, h, 0, 0)),
                   pl.BlockSpec((1, 1, 1, 1), lambda b, h, c: (b, h, 0, 0))],
        out_shape=[jax.ShapeDtypeStruct((batch * seq, heads * dv), BF16),
                   jax.ShapeDtypeStruct((batch, heads, dqk, dv), F32),
                   jax.ShapeDtypeStruct((batch, heads, 1, dqk), F32),
                   jax.ShapeDtypeStruct((batch, heads, 1, 1), F32)],
        compiler_params=_params(("parallel", "parallel", "arbitrary"), blocks,
                                8 * _nbytes((chunk, chunk), F32)),
        name="mlstm_prompt",
    )(b_i, b_f, z, z, z, z, z, gates_row, gain.reshape(heads, 1, dv))


def _mlstm_sample_body(bi_ref, bf_ref, qk_ref, v_ref, o_ref, zm_ref, gain_ref, c_ref, n_ref, m_ref,
                       h_ref, co_ref, no_ref, mo_ref, *, heads, dqk, dv, gate_lane, k_scale):
    qk = qk_ref[0]
    q_rows = qk[0:heads, :]
    k_rows = qk[heads:2 * heads, :] * k_scale
    pad = jnp.zeros((dqk - 2 * heads, dqk), F32)
    cols = jnp.concatenate([q_rows, k_rows, pad], axis=0).T
    zm = zm_ref[0]
    ig = _softcap(zm[:, gate_lane:gate_lane + heads] + bi_ref[...])
    fg = _log_sigmoid(_softcap(zm[:, gate_lane + heads:gate_lane + 2 * heads] + bf_ref[...]))
    loga = fg + m_ref[0]
    m_t = jnp.maximum(loga, ig)
    dm = jnp.exp(ig - m_t)
    a = jnp.exp(loga - m_t)
    floor = jnp.exp(-m_t)
    mo_ref[0] = m_t
    v_all = v_ref[0]
    o_all = o_ref[0]
    n_all = n_ref[0]
    for h in range(heads):
        a_h, dm_h = a[:, h:h + 1], dm[:, h:h + 1]
        q_row, k_row = q_rows[h:h + 1, :], k_rows[h:h + 1, :]
        q_col, k_col = cols[:, h:h + 1], cols[:, heads + h:heads + h + 1]
        v_h, n_h = v_all[h:h + 1, :], n_all[h:h + 1, :]
        c_old = c_ref[0, h]
        s = jnp.sum(q_row * k_row, axis=-1, keepdims=True) * dm_h
        num = a_h * jnp.sum(c_old * q_col, axis=0, keepdims=True) + s * v_h
        den = a_h * jnp.sum(q_row * n_h, axis=-1, keepdims=True) + s
        hh = num / jnp.maximum(jnp.abs(den), floor[:, h:h + 1])
        hh = _rms(hh, gain_ref[h:h + 1, :]) * jax.nn.sigmoid(o_all[h:h + 1, :])
        h_ref[0, :, h * dv:(h + 1) * dv] = hh.astype(h_ref.dtype)
        co_ref[0, h] = a_h * c_old + (dm_h * k_col) * v_h
        no_ref[0, h:h + 1, :] = a_h * n_h + dm_h * k_row


def mlstm_sample(qk, v, o, zmisc, b_i, b_f, gain, state_c, state_n, state_m, layer, lay):
    heads, dqk, dv = lay["ml_heads"], lay["ml_dqk"], lay["ml_dv"]
    db = qk.shape[0]
    blocks = [((heads, dqk, dv), F32)] * 2 + [((2 * heads, dqk), F32), ((heads, dv), F32)]
    body = functools.partial(_mlstm_sample_body, heads=heads, dqk=dqk, dv=dv,
                             gate_lane=lay["misc_gate_lane"], k_scale=dqk ** -0.5)
    return pl.pallas_call(
        body,
        grid=(db,),
        in_specs=[pl.BlockSpec((1, heads), lambda b: (0, 0)),
                  pl.BlockSpec((1, heads), lambda b: (0, 0)),
                  pl.BlockSpec((1, 2 * heads, dqk), lambda b: (b, 0, 0)),
                  pl.BlockSpec((1, heads, dv), lambda b: (b, 0, 0)),
                  pl.BlockSpec((1, heads, dv), lambda b: (b, 0, 0)),
                  pl.BlockSpec((1, 1, LANES), lambda b: (b, 0, 0)),
                  pl.BlockSpec((heads, dv), lambda b: (0, 0)),
                  pl.BlockSpec((None, 1, heads, dqk, dv), lambda b: (layer, b, 0, 0, 0)),
                  pl.BlockSpec((None, 1, heads, dqk), lambda b: (layer, b, 0, 0)),
                  pl.BlockSpec((None, 1, 1, heads), lambda b: (layer, b, 0, 0))],
        out_specs=[pl.BlockSpec((1, 1, heads * dv), lambda b: (b, 0, 0)),
                   pl.BlockSpec((1, heads, dqk, dv), lambda b: (b, 0, 0, 0)),
                   pl.BlockSpec((1, heads, dqk), lambda b: (b, 0, 0)),
                   pl.BlockSpec((1, 1, heads), lambda b: (b, 0, 0))],
        out_shape=[jax.ShapeDtypeStruct((db, 1, heads * dv), BF16),
                   jax.ShapeDtypeStruct((db, heads, dqk, dv), F32),
                   jax.ShapeDtypeStruct((db, heads, dqk), F32),
                   jax.ShapeDtypeStruct((db, 1, heads), F32)],
        compiler_params=_params(("parallel",), blocks),
        name="mlstm_sample",
    )(b_i.reshape(1, heads), b_f.reshape(1, heads), qk, v, o, zmisc, gain,
      state_c, state_n, state_m.reshape(state_m.shape[0], db, 1, heads))


def _kv_prep_body(ckv_ref, zm_ref, g_ref, cos_ref, sin_ref, c32_ref, c16_ref, kr32_ref, kr16_ref, *, rope_dim):
    c = _rms(ckv_ref[...], g_ref[...])
    c32_ref[...] = c
    c16_ref[...] = c.astype(BF16)
    x = zm_ref[...]
    kr = (x * cos_ref[...] + _rope_rotate(x) * sin_ref[...])[:, :rope_dim]
    kr32_ref[...] = kr
    kr16_ref[...] = kr.astype(BF16)


def kv_prep(z, g, cos, sin, lay, *, tm, table_blocks):
    m = z.shape[0]
    kvw, rd = lay["kv_lora"], lay["rope_dim"]
    jc, jm = lay["ckv"] // kvw, lay["misc"] // LANES
    blocks = [((tm, kvw), F32)] * 2 + [((tm, LANES), F32)] * 5
    return pl.pallas_call(
        functools.partial(_kv_prep_body, rope_dim=rd),
        grid=(m // tm,),
        in_specs=[pl.BlockSpec((tm, kvw), lambda i: (i, jc)),
                  pl.BlockSpec((tm, LANES), lambda i: (i, jm)),
                  pl.BlockSpec((1, kvw), lambda i: (0, 0)),
                  pl.BlockSpec((tm, LANES), lambda i: (i % table_blocks, 0)),
                  pl.BlockSpec((tm, LANES), lambda i: (i % table_blocks, 0))],
        out_specs=[pl.BlockSpec((tm, kvw), lambda i: (i, 0)),
                   pl.BlockSpec((tm, kvw), lambda i: (i, 0)),
                   pl.BlockSpec((tm, rd), lambda i: (i, 0)),
                   pl.BlockSpec((tm, rd), lambda i: (i, 0))],
        out_shape=[jax.ShapeDtypeStruct((m, kvw), F32), jax.ShapeDtypeStruct((m, kvw), BF16),
                   jax.ShapeDtypeStruct((m, rd), F32), jax.ShapeDtypeStruct((m, rd), BF16)],
        compiler_params=_params(("parallel",), blocks),
        name="mla_kv_prep",
    )(z, z, g.reshape(1, kvw), cos, sin)


def _q_prep_body(cq_ref, g_ref, wuq_ref, wuk_ref, cos_ref, sin_ref, qlat_ref, qrope_ref,
                 *, heads, nope, rope_dim):
    hq = _rms(cq_ref[...], g_ref[...]).astype(BF16)
    q = _dot(hq, wuq_ref[...])
    for h in range(heads):
        qn = q[:, h * nope:(h + 1) * nope].astype(BF16)
        qlat_ref[h] = _dot(qn, wuk_ref[h]).astype(qlat_ref.dtype)
    cos = cos_ref[...]
    sin = sin_ref[...]
    base = heads * nope
    per_tile = LANES // rope_dim
    for p in range(heads // per_tile):
        x = q[:, base + p * LANES: base + (p + 1) * LANES]
        r = (x * cos + _rope_rotate(x) * sin).astype(qrope_ref.dtype)
        for t in range(per_tile):
            qrope_ref[p * per_tile + t] = r[:, t * rope_dim:(t + 1) * rope_dim]


def q_prep(z, g, wuq, wuk, cos, sin, lay, *, tm, table_blocks):
    m = z.shape[0]
    heads, nope, rd, kvw, ql = lay["mla_heads"], lay["nope"], lay["rope_dim"], lay["kv_lora"], lay["q_lora"]
    jq = lay["cq"] // ql
    blocks = [((tm, ql), F32), ((ql, heads * (nope + rd)), BF16), ((heads, nope, kvw), BF16),
              ((heads, tm, kvw), BF16), ((heads, tm, LANES), BF16)]
    return pl.pallas_call(
        functools.partial(_q_prep_body, heads=heads, nope=nope, rope_dim=rd),
        grid=(m // tm,),
        in_specs=[pl.BlockSpec((tm, ql), lambda i: (i, jq)),
                  pl.BlockSpec((1, ql), lambda i: (0, 0)),
                  pl.BlockSpec((ql, heads * (nope + rd)), lambda i: (0, 0)),
                  pl.BlockSpec((heads, nope, kvw), lambda i: (0, 0, 0)),
                  pl.BlockSpec((tm, LANES), lambda i: (i % table_blocks, 0)),
                  pl.BlockSpec((tm, LANES), lambda i: (i % table_blocks, 0))],
        out_specs=[pl.BlockSpec((heads, tm, kvw), lambda i: (0, i, 0)),
                   pl.BlockSpec((heads, tm, rd), lambda i: (0, i, 0))],
        out_shape=[jax.ShapeDtypeStruct((heads, m, kvw), BF16),
                   jax.ShapeDtypeStruct((heads, m, rd), BF16)],
        compiler_params=_params(("parallel",), blocks, _nbytes((tm, heads * (nope + rd)), F32)),
        name="mla_q_prep",
    )(z, g.reshape(1, ql), wuq, wuk, cos, sin)


def _mla_prompt_body(ql_ref, qr_ref, kc_ref, kr_ref, wuv_ref, o_ref, m_scr, l_scr, acc_scr,
                     *, heads, tq, tk, vdim, scale):
    i = pl.program_id(1)
    kvw = ql_ref.shape[-1]
    q1 = ql_ref[...].reshape(heads * tq, kvw)
    q2 = qr_ref[...].reshape(heads * tq, qr_ref.shape[-1])
    m_scr[...] = jnp.full_like(m_scr, -jnp.inf)
    l_scr[...] = jnp.zeros_like(l_scr)
    acc_scr[...] = jnp.zeros_like(acc_scr)
    q_pos = i * tq + lax.broadcasted_iota(jnp.int32, (tq, tk), 0)
    k_off = lax.broadcasted_iota(jnp.int32, (tq, tk), 1)

    def step(j, carry):
        start = pl.multiple_of(j * tk, tk)
        kc = kc_ref[pl.ds(start, tk), :]
        kr = kr_ref[pl.ds(start, tk), :]
        s = (_dot_nt(q1, kc) + _dot_nt(q2, kr)) * scale
        visible = (start + k_off) <= q_pos
        s = jnp.where(visible[None], s.reshape(heads, tq, tk), -jnp.inf).reshape(heads * tq, tk)
        m_old = m_scr[...]
        m_new = jnp.maximum(m_old, jnp.max(s, axis=-1, keepdims=True))
        alpha = jnp.exp(m_old - m_new)
        p = jnp.exp(s - m_new)
        l_scr[...] = l_scr[...] * alpha + jnp.sum(p, axis=-1, keepdims=True)
        acc_scr[...] = acc_scr[...] * alpha + _dot(p.astype(BF16), kc)
        m_scr[...] = m_new
        return carry

    n_kv = (i * tq + tq + tk - 1) // tk
    lax.fori_loop(0, n_kv, step, 0)
    o = acc_scr[...] / l_scr[...]
    for h in range(heads):
        oh = o[h * tq:(h + 1) * tq, :].astype(BF16)
        o_ref[:, h * vdim:(h + 1) * vdim] = _dot(oh, wuv_ref[h]).astype(o_ref.dtype)


def mla_prompt(qlat, qrope, kc, kr, wuv, lay, *, batch, seq, tq, tk):
    heads, kvw, rd, vdim = lay["mla_heads"], lay["kv_lora"], lay["rope_dim"], lay["v_dim"]
    nq = seq // tq
    blocks = [((heads, tq, kvw), BF16), ((heads, tq, LANES), BF16), ((seq, kvw), BF16), ((seq, LANES), BF16),
              ((heads, kvw, vdim), BF16), ((tq, heads * vdim), BF16)]
    scratch = _nbytes((heads * tq, kvw), F32) + 2 * _nbytes((heads * tq, LANES), F32)
    body = functools.partial(_mla_prompt_body, heads=heads, tq=tq, tk=tk, vdim=vdim,
                             scale=(lay["nope"] + rd) ** -0.5)
    return pl.pallas_call(
        body,
        grid=(batch, nq),
        in_specs=[pl.BlockSpec((heads, tq, kvw), lambda b, i: (0, b * nq + i, 0)),
                  pl.BlockSpec((heads, tq, rd), lambda b, i: (0, b * nq + i, 0)),
                  pl.BlockSpec((seq, kvw), lambda b, i: (b, 0)),
                  pl.BlockSpec((seq, rd), lambda b, i: (b, 0)),
                  pl.BlockSpec((heads, kvw, vdim), lambda b, i: (0, 0, 0))],
        out_specs=pl.BlockSpec((tq, heads * vdim), lambda b, i: (b * nq + i, 0)),
        out_shape=jax.ShapeDtypeStruct((batch * seq, heads * vdim), BF16),
        scratch_shapes=[pltpu.VMEM((heads * tq, 1), F32), pltpu.VMEM((heads * tq, 1), F32),
                        pltpu.VMEM((heads * tq, kvw), F32)],
        compiler_params=_params(("parallel", "arbitrary"), blocks,
                                scratch + 4 * _nbytes((heads * tq, tk), F32)),
        name="mla_prompt",
    )(qlat, qrope, kc, kr, wuv)


def _mla_sample_body(pt_ref, ql_ref, qr_ref, cn_ref, krn_ref, *rest, pages, scale):
    lat_refs = rest[:pages]
    kr_refs = rest[pages:2 * pages]
    o_ref, m_scr, l_scr, acc_scr = rest[2 * pages:]
    p_idx = pl.program_id(1)
    ql = ql_ref[0]
    qr = qr_ref[0]

    @pl.when(p_idx == 0)
    def _():
        cn = cn_ref[0].astype(F32)
        krn = krn_ref[0].astype(F32)
        s0 = (jnp.sum(ql.astype(F32) * cn, axis=-1, keepdims=True)
              + jnp.sum(qr.astype(F32) * krn, axis=-1, keepdims=True)) * scale
        m_scr[...] = s0
        l_scr[...] = jnp.ones_like(l_scr)
        acc_scr[...] = jnp.broadcast_to(cn, acc_scr.shape)

    cs = [r[...].astype(BF16) for r in lat_refs]
    ss = [(_dot_nt(ql, c) + _dot_nt(qr, r[...].astype(BF16))) * scale for c, r in zip(cs, kr_refs)]
    m_old = m_scr[...]
    m_new = m_old
    for s in ss:
        m_new = jnp.maximum(m_new, jnp.max(s, axis=-1, keepdims=True))
    alpha = jnp.exp(m_old - m_new)
    l_new = l_scr[...] * alpha
    acc = acc_scr[...] * alpha
    for s, c in zip(ss, cs):
        p = jnp.exp(s - m_new)
        l_new = l_new + jnp.sum(p, axis=-1, keepdims=True)
        acc = acc + _dot(p.astype(BF16), c)
    m_scr[...] = m_new
    l_scr[...] = l_new
    acc_scr[...] = acc

    @pl.when(p_idx == pl.num_programs(1) - 1)
    def _():
        o_ref[0] = (acc / l_new).astype(o_ref.dtype)


def mla_sample(qlat, qrope, c_new, kr_new, cache_lat, cache_kr, page_table, layer, lay, *, pages):
    db, heads, kvw = qlat.shape
    rd = qrope.shape[-1]
    n_pages = page_table.shape[1]
    page = cache_lat.shape[2]
    steps = n_pages // pages
    pt_flat = page_table.reshape(-1)

    def page_map(j):
        return lambda b, p, pt: (layer, pt[b * n_pages + p * pages + j], 0, 0)

    blocks = ([((page, kvw), F32)] * pages + [((page, LANES), F32)] * pages
              + [((heads, kvw), BF16), ((heads, LANES), BF16), ((heads, kvw), F32)])
    body = functools.partial(_mla_sample_body, pages=pages, scale=(lay["nope"] + rd) ** -0.5)
    grid_spec = pltpu.PrefetchScalarGridSpec(
        num_scalar_prefetch=1,
        grid=(db, steps),
        in_specs=([pl.BlockSpec((1, heads, kvw), lambda b, p, pt: (b, 0, 0)),
                   pl.BlockSpec((1, heads, rd), lambda b, p, pt: (b, 0, 0)),
                   pl.BlockSpec((1, 1, kvw), lambda b, p, pt: (b, 0, 0)),
                   pl.BlockSpec((1, 1, rd), lambda b, p, pt: (b, 0, 0))]
                  + [pl.BlockSpec((None, None, page, kvw), page_map(j)) for j in range(pages)]
                  + [pl.BlockSpec((None, None, page, rd), page_map(j)) for j in range(pages)]),
        out_specs=pl.BlockSpec((1, heads, kvw), lambda b, p, pt: (b, 0, 0)),
        scratch_shapes=[pltpu.VMEM((heads, 1), F32), pltpu.VMEM((heads, 1), F32),
                        pltpu.VMEM((heads, kvw), F32)])
    return pl.pallas_call(
        body,
        grid_spec=grid_spec,
        out_shape=jax.ShapeDtypeStruct((db, heads, kvw), BF16),
        compiler_params=_params(("parallel", "arbitrary"), blocks),
        name="mla_sample",
    )(pt_flat, qlat, qrope, c_new, kr_new, *([cache_lat] * pages), *([cache_kr] * pages))


def _uv_body(o_ref, w_ref, h_ref):
    h_ref[...] = _dot(o_ref[0], w_ref[0]).astype(h_ref.dtype)


def mla_up_v(o_lat, wuv):
    heads, m, kvw = o_lat.shape
    vdim = wuv.shape[-1]
    blocks = [((m, kvw), BF16), ((kvw, vdim), BF16), ((m, vdim), BF16)]
    return pl.pallas_call(
        _uv_body,
        grid=(heads,),
        in_specs=[pl.BlockSpec((1, m, kvw), lambda h: (h, 0, 0)),
                  pl.BlockSpec((1, kvw, vdim), lambda h: (h, 0, 0))],
        out_specs=pl.BlockSpec((m, vdim), lambda h: (0, h)),
        out_shape=jax.ShapeDtypeStruct((m, heads * vdim), BF16),
        compiler_params=_params(("parallel",), blocks),
        name="mla_up_v",
    )(o_lat, wuv)


def _mem_attend_body(q_ref, k_ref, v_ref, o_ref, *, heads, dim, rows, squeeze):
    q = q_ref[0] if squeeze else q_ref[...]
    if q.shape[0] < rows:
        q = jnp.broadcast_to(q, (rows, q.shape[1]))
    k = k_ref[...].astype(BF16)
    v = v_ref[...].astype(BF16)
    scale = dim ** -0.5
    for h in range(heads):
        sl = slice(h * dim, (h + 1) * dim)
        s = _dot_nt(q[:, sl], k[:, sl]) * scale
        p = jnp.exp(s - jnp.max(s, axis=-1, keepdims=True))
        p = p / jnp.sum(p, axis=-1, keepdims=True)
        o = _dot(p.astype(BF16), v[:, sl]).astype(o_ref.dtype)
        if squeeze:
            o_ref[0, :, sl] = o[0:1, :]
        else:
            o_ref[:, sl] = o


def mem_attend_prompt(q, kv, lay, *, batch, seq, tq):
    heads, dim, n_mem = lay["mem_heads"], lay["mem_dim"], lay["n_mem"]
    w = heads * dim
    nq = seq // tq
    blocks = [((tq, w), BF16), ((n_mem, w), F32), ((n_mem, w), F32), ((tq, w), BF16)]
    body = functools.partial(_mem_attend_body, heads=heads, dim=dim, rows=tq, squeeze=False)
    return pl.pallas_call(
        body,
        grid=(batch, nq),
        in_specs=[pl.BlockSpec((tq, w), lambda b, i: (b * nq + i, 0)),
                  pl.BlockSpec((n_mem, w), lambda b, i: (b, 0)),
                  pl.BlockSpec((n_mem, w), lambda b, i: (b, 1))],
        out_specs=pl.BlockSpec((tq, w), lambda b, i: (b * nq + i, 0)),
        out_shape=jax.ShapeDtypeStruct((batch * seq, w), BF16),
        compiler_params=_params(("parallel", "parallel"), blocks, 4 * _nbytes((tq, n_mem), F32)),
        name="mem_attend_prompt",
    )(q, kv, kv)


def mem_attend_sample(q, cache_k, cache_v, layer, lay):
    heads, dim, n_mem = lay["mem_heads"], lay["mem_dim"], lay["n_mem"]
    w = heads * dim
    db = q.shape[0]
    blocks = [((n_mem, w), F32), ((n_mem, w), F32)]
    body = functools.partial(_mem_attend_body, heads=heads, dim=dim, rows=8, squeeze=True)
    return pl.pallas_call(
        body,
        grid=(db,),
        in_specs=[pl.BlockSpec((1, 1, w), lambda b: (b, 0, 0)),
                  pl.BlockSpec((None, None, n_mem, w), lambda b: (layer, b, 0, 0)),
                  pl.BlockSpec((None, None, n_mem, w), lambda b: (layer, b, 0, 0))],
        out_specs=pl.BlockSpec((1, 1, w), lambda b: (b, 0, 0)),
        out_shape=jax.ShapeDtypeStruct((db, 1, w), BF16),
        compiler_params=_params(("parallel",), blocks),
        name="mem_attend_sample",
    )(q, cache_k, cache_v)


def _layout(shapes):
    lay = dict(shapes)
    qk_w = lay["ml_heads"] * lay["ml_dqk"]
    v_w = lay["ml_heads"] * lay["ml_dv"]
    d = lay["d_model"]
    off = 0
    for name, width in (("q", qk_w), ("k", qk_w), ("v", v_w), ("o", v_w), ("cq", lay["q_lora"]),
                        ("ga", d), ("gb", d), ("ckv", lay["kv_lora"]), ("misc", LANES)):
        lay[name] = off
        off += width
    lay["z_cols"] = off
    lay["misc_gate_lane"] = lay["rope_dim"]
    assert lay["rope_dim"] + 2 * lay["ml_heads"] <= LANES
    return lay


def _repack_w_in(w, lay):
    heads = lay["ml_heads"]
    qk_w = heads * lay["ml_dqk"]
    v_w = heads * lay["ml_dv"]
    d = lay["d_model"]
    o = 0
    src = {}
    for name, width in (("q", qk_w), ("k", qk_w), ("v", v_w), ("i", heads), ("f", heads), ("o", v_w),
                        ("cq", lay["q_lora"]), ("ckv", lay["kv_lora"]), ("kr", lay["rope_dim"]),
                        ("ga", d), ("gb", d)):
        src[name] = (o, o + width)
        o += width
    pad = LANES - lay["rope_dim"] - 2 * heads
    parts = [w[:, src[n][0]:src[n][1]] for n in ("q", "k", "v", "o", "cq", "ga", "gb", "ckv", "kr", "i", "f")]
    parts.append(jnp.zeros((w.shape[0], pad), w.dtype))
    return jnp.concatenate(parts, axis=1).astype(BF16)


def _repack_w_uq(w, lay):
    heads, nope, rd = lay["mla_heads"], lay["nope"], lay["rope_dim"]
    w3 = w.reshape(w.shape[0], heads, nope + rd)
    return jnp.concatenate([w3[:, :, :nope].reshape(w.shape[0], heads * nope),
                            w3[:, :, nope:].reshape(w.shape[0], heads * rd)], axis=1).astype(BF16)


def _rope_tables(pos, rope_dim):
    half = rope_dim // 2
    inv = ROPE_THETA ** (-jnp.arange(half, dtype=F32) / half)
    ang = pos.astype(F32)[:, None] * inv[None, :]
    reps = LANES // half
    return jnp.tile(jnp.cos(ang), (1, reps)), jnp.tile(jnp.sin(ang), (1, reps))


def _tile(n, want):
    if n <= want:
        return n
    t = want
    while n % t:
        t -= 8
    return t


def _mixer_common(x, lw, lay, tiles):
    tm, tn = tiles["tm"], tiles["tn"]
    h = rmsnorm(x, lw["norm_ffn1"], BF16, tm=tiles["tr"])
    u = swiglu_up(h, lw["ffn1_w_gate"], lw["ffn1_w_up"], tm=tm, tn=tn)
    x = matmul_residual(u, lw["ffn1_w_down"], x, 0.5, tm=tm, tn=tn)
    h = rmsnorm(x, lw["norm_mix"], BF16, tm=tiles["tr"])
    z = matmul(h, lw["w_in"], F32, tm=tm, tn=tiles["tn_in"])
    return x, z


def _layer_tail(x, z, h_m, h_a, lw, lay, tiles, attend):
    tm, tn = tiles["tm"], tiles["tn"]
    merged = gated_merge(h_m, h_a, lw["w_branch_mlstm"], lw["w_branch_mla"], z, lay["ga"], lay["gb"], tm=tm, tn=tn)
    x = matmul_residual(merged, lw["w_out"], x, 1.0, tm=tm, tn=tn)
    h = rmsnorm(x, lw["norm_mem"], BF16, tm=tiles["tr"])
    q = matmul(h, lw["mem_w_q"], BF16, tm=tm, tn=tn)
    o = attend(q)
    x = matmul_residual(o, lw["mem_w_o"], x, 1.0, tm=tm, tn=tn)
    h = rmsnorm(x, lw["norm_ffn2"], BF16, tm=tiles["tr"])
    u = swiglu_up(h, lw["ffn2_w_gate"], lw["ffn2_w_up"], tm=tm, tn=tn)
    return matmul_residual(u, lw["ffn2_w_down"], x, 0.5, tm=tm, tn=tn)


def kernel(x_prompt, mem_prompt, x_sample, state_mlstm_C, state_mlstm_n, state_mlstm_m, cache_mla_latent, cache_mla_krope, cache_mem_k, cache_mem_v, page_table, norm_ffn1, ffn1_w_gate, ffn1_w_up, ffn1_w_down, norm_mix, w_in, mlstm_b_i, mlstm_b_f, mlstm_norm, mla_norm_q, mla_w_uq, mla_norm_kv, mla_w_uk, mla_w_uv, w_branch_mlstm, w_branch_mla, w_out, norm_mem, norm_mem_src, mem_w_q, mem_w_k, mem_w_v, mem_w_o, norm_ffn2, ffn2_w_gate, ffn2_w_up, ffn2_w_down, norm_final):
    batch, seq, d_model = x_prompt.shape
    db, dec_seq, _ = x_sample.shape
    assert dec_seq == 1
    depth = norm_ffn1.shape[0]
    n_mem = mem_prompt.shape[1]
    mla_heads, nope = mla_w_uk.shape[2], mla_w_uk.shape[3]
    lay = _layout(dict(
        d_model=d_model, ml_heads=state_mlstm_C.shape[2], ml_dqk=state_mlstm_C.shape[3],
        ml_dv=state_mlstm_C.shape[4], mla_heads=mla_heads, nope=nope,
        q_lora=mla_w_uq.shape[1], kv_lora=mla_w_uk.shape[1], rope_dim=cache_mla_krope.shape[3],
        v_dim=mla_w_uv.shape[3], mem_heads=cache_mem_k.shape[3], mem_dim=cache_mem_k.shape[4], n_mem=n_mem))
    heads, dqk, dv = lay["ml_heads"], lay["ml_dqk"], lay["ml_dv"]
    mem_w = lay["mem_heads"] * lay["mem_dim"]
    rd, kvw = lay["rope_dim"], lay["kv_lora"]
    mp = batch * seq

    chunk = min(256, seq)
    tq_mla = min(128, seq)
    tiles_p = dict(tm=_tile(mp, 1024), tn=512, tn_in=_tile(lay["z_cols"], 640) if lay["z_cols"] % 640 == 0 else 128,
                   tr=_tile(mp, 256))
    tiles_s = dict(tm=db, tn=512, tn_in=tiles_p["tn_in"], tr=db)

    past_len = page_table.shape[1] * cache_mla_latent.shape[2]
    cos_p, sin_p = _rope_tables(jnp.arange(seq), rd)
    cos_s, sin_s = _rope_tables(jnp.full((db,), past_len), rd)
    tm_prep_p = _tile(seq, 256)

    cache_k4 = cache_mem_k.reshape(depth, db, n_mem, mem_w)
    cache_v4 = cache_mem_v.reshape(depth, db, n_mem, mem_w)

    xp = x_prompt.reshape(mp, d_model)
    xs = x_sample.reshape(db, d_model)
    out_p = [[] for _ in range(7)]
    out_s = [[] for _ in range(5)]
    for l in range(depth):
        lw = dict(
            norm_ffn1=norm_ffn1[l], ffn1_w_gate=ffn1_w_gate[l].astype(BF16), ffn1_w_up=ffn1_w_up[l].astype(BF16),
            ffn1_w_down=ffn1_w_down[l].astype(BF16), norm_mix=norm_mix[l], w_in=_repack_w_in(w_in[l], lay),
            w_branch_mlstm=w_branch_mlstm[l].astype(BF16), w_branch_mla=w_branch_mla[l].astype(BF16),
            w_out=w_out[l].astype(BF16), norm_mem=norm_mem[l], mem_w_q=mem_w_q[l].astype(BF16),
            mem_w_o=mem_w_o[l].astype(BF16), norm_ffn2=norm_ffn2[l], ffn2_w_gate=ffn2_w_gate[l].astype(BF16),
            ffn2_w_up=ffn2_w_up[l].astype(BF16), ffn2_w_down=ffn2_w_down[l].astype(BF16))
        wuq = _repack_w_uq(mla_w_uq[l], lay)
        wuk = jnp.transpose(mla_w_uk[l], (1, 2, 0)).astype(BF16)
        wuv = jnp.transpose(mla_w_uv[l], (1, 0, 2)).astype(BF16)
        w_mem_kv = jnp.concatenate([mem_w_k[l], mem_w_v[l]], axis=1).astype(BF16)

        hm_src = rmsnorm(mem_prompt.reshape(batch * n_mem, d_model), norm_mem_src[l], BF16, tm=_tile(batch * n_mem, 256))
        mem_kv = matmul(hm_src, w_mem_kv, F32, tm=_tile(batch * n_mem, 1024), tn=512)
        xp, zp = _mixer_common(xp, lw, lay, tiles_p)
        gate_cols = zp[:, lay["misc"] + rd: lay["misc"] + rd + 2 * heads]
        gates_row = gate_cols.reshape(batch, seq, 2, heads).transpose(0, 3, 2, 1)
        h_m, p_c, p_n, p_m = mlstm_prompt(zp, gates_row, mlstm_b_i[l], mlstm_b_f[l], mlstm_norm[l], lay,
                                          batch=batch, seq=seq, chunk=chunk)
        c32, c16, kr32, kr16 = kv_prep(zp, mla_norm_kv[l], cos_p, sin_p, lay, tm=tm_prep_p,
                                       table_blocks=seq // tm_prep_p)
        qlat, qrope = q_prep(zp, mla_norm_q[l], wuq, wuk, cos_p, sin_p, lay, tm=tm_prep_p,
                             table_blocks=seq // tm_prep_p)
        h_a = mla_prompt(qlat, qrope, c16, kr16, wuv, lay, batch=batch, seq=seq, tq=tq_mla, tk=min(256, seq))
        attend_p = functools.partial(mem_attend_prompt, kv=mem_kv, lay=lay, batch=batch, seq=seq, tq=_tile(seq, 512))
        xp = _layer_tail(xp, zp, h_m, h_a, lw, lay, tiles_p, attend_p)
        for lst, a in zip(out_p, (p_c, p_n.reshape(batch, heads, dqk), p_m.reshape(batch, heads),
                                  c32.reshape(batch, seq, kvw), kr32.reshape(batch, seq, rd),
                                  mem_kv[:, :mem_w].reshape(batch, n_mem, lay["mem_heads"], lay["mem_dim"]),
                                  mem_kv[:, mem_w:].reshape(batch, n_mem, lay["mem_heads"], lay["mem_dim"]))):
            lst.append(a)

        xs, zs = _mixer_common(xs, lw, lay, tiles_s)
        qk_s = zs[:, lay["q"]:lay["v"]].reshape(db, 2 * heads, dqk)
        v_s = zs[:, lay["v"]:lay["o"]].reshape(db, heads, dv)
        o_s = zs[:, lay["o"]:lay["cq"]].reshape(db, heads, dv)
        misc_s = zs[:, lay["misc"]:].reshape(db, 1, LANES)
        h_m, s_c, s_n, s_m = mlstm_sample(qk_s, v_s, o_s, misc_s, mlstm_b_i[l], mlstm_b_f[l], mlstm_norm[l],
                                          state_mlstm_C, state_mlstm_n, state_mlstm_m, l, lay)
        c32, c16, kr32, kr16 = kv_prep(zs, mla_norm_kv[l], cos_s, sin_s, lay, tm=db, table_blocks=1)
        qlat, qrope = q_prep(zs, mla_norm_q[l], wuq, wuk, cos_s, sin_s, lay, tm=db, table_blocks=1)
        o_lat = mla_sample(qlat.transpose(1, 0, 2), qrope.transpose(1, 0, 2), c16.reshape(db, 1, kvw),
                           kr16.reshape(db, 1, rd), cache_mla_latent, cache_mla_krope, page_table, l, lay,
                           pages=min(8, page_table.shape[1]))
        h_a = mla_up_v(o_lat.transpose(1, 0, 2), wuv)
        attend_s = lambda q: mem_attend_sample(q.reshape(db, 1, mem_w), cache_k4, cache_v4, l, lay).reshape(db, mem_w)
        xs = _layer_tail(xs, zs, h_m.reshape(db, heads * dv), h_a, lw, lay, tiles_s, attend_s)
        for lst, a in zip(out_s, (s_c, s_n, s_m.reshape(db, heads), c32.reshape(db, 1, kvw), kr32.reshape(db, 1, rd))):
            lst.append(a)

    y_prompt = rmsnorm(xp, norm_final, F32, tm=tiles_p["tr"]).reshape(batch, seq, d_model)
    y_sample = rmsnorm(xs, norm_final, F32, tm=db).reshape(db, 1, d_model)
    return (y_prompt, y_sample) + tuple(jnp.stack(a) for a in out_p) + tuple(jnp.stack(a) for a in out_s)
```

```python
import functools
import math

import jax
import jax.numpy as jnp
from jax import lax
from jax.experimental import pallas as pl
from jax.experimental.pallas import tpu as pltpu

F32 = jnp.float32
BF16 = jnp.bfloat16

EPS = 1e-6
GATE_CAP = 15.0
ROPE_THETA = 10000.0

LANES = 128
BF16_ROWS = 16
VMEM_BYTES = 64 * 1024 * 1024
VMEM_TEMP_BYTES = 12 * 1024 * 1024


def _nbytes(shape, dtype):
    return math.prod(shape) * jnp.dtype(dtype).itemsize


def _params(semantics, blocks, scratch_bytes=0):
    need = 2 * sum(_nbytes(s, d) for s, d in blocks) + scratch_bytes + VMEM_TEMP_BYTES
    return pltpu.CompilerParams(dimension_semantics=semantics,
                                vmem_limit_bytes=min(need, VMEM_BYTES - 4 * 1024 * 1024))


def _dot(a, b):
    return jnp.dot(a, b, preferred_element_type=F32)


def _dot_nt(a, b):
    return lax.dot_general(a, b, (((1,), (1,)), ((), ())), preferred_element_type=F32)


def _dot_tn(a, b):
    return lax.dot_general(a, b, (((0,), (0,)), ((), ())), preferred_element_type=F32)


def _softcap(a):
    return GATE_CAP * jnp.tanh(a / GATE_CAP)


def _log_sigmoid(a):
    return jnp.minimum(a, 0.0) - jnp.log1p(jnp.exp(-jnp.abs(a)))


def _rms(x, g):
    return x * lax.rsqrt(jnp.mean(x * x, axis=-1, keepdims=True) + EPS) * g


def _rope_rotate(x):
    lane = lax.broadcasted_iota(jnp.int32, x.shape, x.ndim - 1)
    first_half = (lane % 64) < 32
    return jnp.where(first_half, -pltpu.roll(x, 96, x.ndim - 1), pltpu.roll(x, 32, x.ndim - 1))


def _rmsnorm_body(x_ref, g_ref, o_ref):
    o_ref[...] = _rms(x_ref[...].astype(F32), g_ref[...].astype(F32)).astype(o_ref.dtype)


def rmsnorm(x, g, out_dtype, *, tm, row0=0, rows=None):
    w = x.shape[1]
    rows = x.shape[0] if rows is None else rows
    i0 = row0 // tm
    assert i0 * tm == row0 and rows % tm == 0
    blocks = [((tm, w), x.dtype), ((tm, w), out_dtype)]
    return pl.pallas_call(
        _rmsnorm_body,
        grid=(rows // tm,),
        in_specs=[pl.BlockSpec((tm, w), lambda i: (i0 + i, 0)),
                  pl.BlockSpec((1, w), lambda i: (0, 0))],
        out_specs=pl.BlockSpec((tm, w), lambda i: (i, 0)),
        out_shape=jax.ShapeDtypeStruct((rows, w), out_dtype),
        compiler_params=_params(("parallel",), blocks),
        name="rmsnorm",
    )(x, g.reshape(1, w))


def _weight_spec(w, layer, tn):
    k = w.shape[-2]
    if w.ndim == 3:
        return pl.BlockSpec((None, k, tn), lambda j, i: (layer, 0, j))
    return pl.BlockSpec((k, tn), lambda j, i: (0, j))


def _weight_scratch(w, tn):
    return [] if w.dtype == BF16 else [pltpu.VMEM((w.shape[-2], tn), BF16)]


def _weight_bytes(w, tn):
    k = w.shape[-2]
    return 2 * _nbytes((k, tn), w.dtype) + (0 if w.dtype == BF16 else _nbytes((k, tn), BF16))


def _resident_bf16(w_ref, w16_ref):
    if w16_ref is None:
        return w_ref[...]

    @pl.when(pl.program_id(1) == 0)
    def _():
        w16_ref[...] = w_ref[...].astype(BF16)

    return w16_ref[...]


def _mm_body(a_ref, w_ref, o_ref, *scratch):
    w = _resident_bf16(w_ref, scratch[0] if scratch else None)
    o_ref[...] = _dot(a_ref[...], w).astype(o_ref.dtype)


def matmul(a, w, out_dtype, *, tm, tn, layer=0):
    m, k = a.shape
    n = w.shape[-1]
    vmem = 2 * _nbytes((tm, k), a.dtype) + _weight_bytes(w, tn) + 3 * _nbytes((tm, tn), F32)
    return pl.pallas_call(
        _mm_body,
        grid=(pl.cdiv(n, tn), m // tm),
        in_specs=[pl.BlockSpec((tm, k), lambda j, i: (i, 0)), _weight_spec(w, layer, tn)],
        out_specs=pl.BlockSpec((tm, tn), lambda j, i: (i, j)),
        out_shape=jax.ShapeDtypeStruct((m, n), out_dtype),
        scratch_shapes=_weight_scratch(w, tn),
        compiler_params=_params(("parallel", "arbitrary"), [], vmem),
        name="matmul",
    )(a, w)


def _mm_residual_body(a_ref, w_ref, r_ref, o_ref, *scratch, scale):
    w = _resident_bf16(w_ref, scratch[0] if scratch else None)
    o_ref[...] = r_ref[...] + scale * _dot(a_ref[...], w)


def matmul_residual(a, w, res, scale, *, tm, tn, layer=0):
    m, k = a.shape
    n = w.shape[-1]
    vmem = 2 * _nbytes((tm, k), a.dtype) + _weight_bytes(w, tn) + 5 * _nbytes((tm, tn), F32)
    return pl.pallas_call(
        functools.partial(_mm_residual_body, scale=scale),
        grid=(n // tn, m // tm),
        in_specs=[pl.BlockSpec((tm, k), lambda j, i: (i, 0)), _weight_spec(w, layer, tn),
                  pl.BlockSpec((tm, tn), lambda j, i: (i, j))],
        out_specs=pl.BlockSpec((tm, tn), lambda j, i: (i, j)),
        out_shape=jax.ShapeDtypeStruct((m, n), F32),
        scratch_shapes=_weight_scratch(w, tn),
        compiler_params=_params(("parallel", "arbitrary"), [], vmem),
        name="matmul_residual",
    )(a, w, res)


def _swiglu_body(a_ref, wg_ref, wu_ref, o_ref, wg16_ref, wu16_ref):
    a = a_ref[...]
    g = _dot(a, _resident_bf16(wg_ref, wg16_ref))
    u = _dot(a, _resident_bf16(wu_ref, wu16_ref))
    o_ref[...] = (g * jax.nn.sigmoid(g) * u).astype(o_ref.dtype)


def swiglu_up(a, wg, wu, *, tm, tn, layer):
    m, k = a.shape
    n = wg.shape[-1]
    vmem = 2 * _nbytes((tm, k), a.dtype) + 2 * _weight_bytes(wg, tn) + 4 * _nbytes((tm, tn), F32)
    return pl.pallas_call(
        _swiglu_body,
        grid=(pl.cdiv(n, tn), m // tm),
        in_specs=[pl.BlockSpec((tm, k), lambda j, i: (i, 0)),
                  _weight_spec(wg, layer, tn), _weight_spec(wu, layer, tn)],
        out_specs=pl.BlockSpec((tm, tn), lambda j, i: (i, j)),
        out_shape=jax.ShapeDtypeStruct((m, n), BF16),
        scratch_shapes=_weight_scratch(wg, tn) + _weight_scratch(wu, tn),
        compiler_params=_params(("parallel", "arbitrary"), [], vmem),
        name="swiglu_up",
    )(a, wg, wu)


def _merge_body(hm_ref, ha_ref, wa_ref, wb_ref, ga_ref, gb_ref, o_ref, wa16_ref, wb16_ref):
    ya = _dot(hm_ref[...], _resident_bf16(wa_ref, wa16_ref))
    yb = _dot(ha_ref[...], _resident_bf16(wb_ref, wb16_ref))
    o_ref[...] = (jax.nn.sigmoid(ga_ref[...]) * ya + jax.nn.sigmoid(gb_ref[...]) * yb).astype(o_ref.dtype)


def gated_merge(hm, ha, wa, wb, z, ga_col, gb_col, *, tm, tn, layer):
    m, ka = hm.shape
    kb = ha.shape[1]
    n = wa.shape[-1]
    ja, jb = ga_col // tn, gb_col // tn
    assert ja * tn == ga_col and jb * tn == gb_col
    vmem = (2 * _nbytes((tm, ka + kb), BF16) + _weight_bytes(wa, tn) + _weight_bytes(wb, tn)
            + 8 * _nbytes((tm, tn), F32))
    return pl.pallas_call(
        _merge_body,
        grid=(n // tn, m // tm),
        in_specs=[pl.BlockSpec((tm, ka), lambda j, i: (i, 0)),
                  pl.BlockSpec((tm, kb), lambda j, i: (i, 0)),
                  _weight_spec(wa, layer, tn), _weight_spec(wb, layer, tn),
                  pl.BlockSpec((tm, tn), lambda j, i: (i, ja + j)),
                  pl.BlockSpec((tm, tn), lambda j, i: (i, jb + j))],
        out_specs=pl.BlockSpec((tm, tn), lambda j, i: (i, j)),
        out_shape=jax.ShapeDtypeStruct((m, n), BF16),
        scratch_shapes=_weight_scratch(wa, tn) + _weight_scratch(wb, tn),
        compiler_params=_params(("parallel", "arbitrary"), [], vmem),
        name="gated_merge",
    )(hm, ha, wa, wb, z, z)


def _mlstm_prompt_body(bi_ref, bf_ref, q_ref, k_ref, v_ref, o_ref, zm_ref, grow_ref, gain_ref,
                       h_ref, c_ref, n_ref, m_ref, *, chunk, heads, gate_lane, k_scale):
    hd = pl.program_id(1)

    @pl.when(pl.program_id(2) == 0)
    def _():
        c_ref[...] = jnp.zeros_like(c_ref)
        n_ref[...] = jnp.zeros_like(n_ref)
        m_ref[...] = jnp.zeros_like(m_ref)

    b_i = bi_ref[hd]
    b_f = bf_ref[hd]
    zm = zm_ref[...]
    lane = lax.broadcasted_iota(jnp.int32, zm.shape, 1)
    i_col = jnp.sum(jnp.where(lane == gate_lane + hd, zm, 0.0), axis=-1, keepdims=True)
    f_col = jnp.sum(jnp.where(lane == gate_lane + heads + hd, zm, 0.0), axis=-1, keepdims=True)
    grow = grow_ref[0, 0]
    ig_col = _softcap(i_col + b_i)
    fg_col = _log_sigmoid(_softcap(f_col + b_f))
    ig_row = _softcap(grow[0:1, :] + b_i)
    fg_row = _log_sigmoid(_softcap(grow[1:2, :] + b_f))

    t_idx = lax.broadcasted_iota(jnp.int32, (chunk, chunk), 0)
    s_idx = lax.broadcasted_iota(jnp.int32, (chunk, chunk), 1)
    causal = s_idx <= t_idx
    b_col = jnp.sum(jnp.where(causal, fg_row, 0.0), axis=-1, keepdims=True)
    b_row = jnp.sum(jnp.where(t_idx <= s_idx, fg_col, 0.0), axis=0, keepdims=True)

    m_prev = m_ref[0, 0]
    logd = jnp.where(causal, b_col - b_row + ig_row, -jnp.inf)
    loga = b_col + m_prev
    m_t = jnp.maximum(loga, jnp.max(logd, axis=-1, keepdims=True))
    dmat = jnp.exp(logd - m_t)
    a = jnp.exp(loga - m_t)

    q = q_ref[...]
    k = k_ref[...] * k_scale
    v = v_ref[...]
    qb, kb, vb = q.astype(BF16), k.astype(BF16), v.astype(BF16)
    c_old = c_ref[0, 0]
    n_old = n_ref[0, 0]
    s = _dot_nt(qb, kb) * dmat
    num = a * _dot(qb, c_old.astype(BF16)) + _dot(s.astype(BF16), vb)
    den = a * jnp.sum(q * n_old, axis=-1, keepdims=True) + jnp.sum(s, axis=-1, keepdims=True)
    h = num / jnp.maximum(jnp.abs(den), jnp.exp(-m_t))
    h = _rms(h, gain_ref[0]) * jax.nn.sigmoid(o_ref[...])
    h_ref[...] = h.astype(h_ref.dtype)

    b_end = b_col[chunk - 1:chunk, :]
    m_new = m_t[chunk - 1:chunk, :]
    a_end = a[chunk - 1:chunk, :]
    w_col = jnp.exp(b_end - b_col + ig_col - m_new)
    c_ref[0, 0] = a_end * c_old + _dot_tn(kb, (w_col * v).astype(BF16))
    n_ref[0, 0] = a_end * n_old + jnp.sum(w_col * k, axis=0, keepdims=True)
    m_ref[0, 0] = m_new


def mlstm_prompt(z, gates_row, b_i, b_f, gain, lay, *, batch, seq, chunk):
    heads, dqk, dv = lay["ml_heads"], lay["ml_dqk"], lay["ml_dv"]
    nc = seq // chunk
    kq, kk = lay["q"] // dqk, lay["k"] // dqk
    kv, ko = lay["v"] // dv, lay["o"] // dv
    kmisc = lay["misc"] // LANES
    row = lambda b, h, c: b * nc + c
    blocks = [((chunk, dqk), F32)] * 2 + [((chunk, dv), F32)] * 2 + [((chunk, LANES), F32),
              ((chunk, dv), BF16), ((dqk, dv), F32)]
    smem = pl.BlockSpec(memory_space=pltpu.SMEM)
    body = functools.partial(_mlstm_prompt_body, chunk=chunk, heads=heads,
                             gate_lane=lay["misc_gate_lane"], k_scale=dqk ** -0.5)
    return pl.pallas_call(
        body,
        grid=(batch, heads, nc),
        in_specs=[smem, smem,
                  pl.BlockSpec((chunk, dqk), lambda b, h, c: (row(b, h, c), kq + h)),
                  pl.BlockSpec((chunk, dqk), lambda b, h, c: (row(b, h, c), kk + h)),
                  pl.BlockSpec((chunk, dv), lambda b, h, c: (row(b, h, c), kv + h)),
                  pl.BlockSpec((chunk, dv), lambda b, h, c: (row(b, h, c), ko + h)),
                  pl.BlockSpec((chunk, LANES), lambda b, h, c: (row(b, h, c), kmisc)),
                  pl.BlockSpec((1, 1, 2, chunk), lambda b, h, c: (b, h, 0, c)),
                  pl.BlockSpec((1, 1, dv), lambda b, h, c: (h, 0, 0))],
        out_specs=[pl.BlockSpec((chunk, dv), lambda b, h, c: (row(b, h, c), h)),
                   pl.BlockSpec((1, 1, dqk, dv), lambda b, h, c: (b, h, 0, 0)),
                   pl.BlockSpec((1, 1, 1, dqk), lambda b, h, c: (b, h, 0, 0)),
                   pl.BlockSpec((1, 1, 1, 1), lambda b, h, c: (b, h, 0, 0))],
        out_shape=[jax.ShapeDtypeStruct((batch * seq, heads * dv), BF16),
                   jax.ShapeDtypeStruct((batch, heads, dqk, dv), F32),
                   jax.ShapeDtypeStruct((batch, heads, 1, dqk), F32),
                   jax.ShapeDtypeStruct((batch, heads, 1, 1), F32)],
        compiler_params=_params(("parallel", "parallel", "arbitrary"), blocks,
                                8 * _nbytes((chunk, chunk), F32)),
        name="mlstm_prompt",
    )(b_i, b_f, z, z, z, z, z, gates_row, gain.reshape(heads, 1, dv))


def _mlstm_sample_body(bi_ref, bf_ref, qk_ref, v_ref, o_ref, zm_ref, gain_ref, c_ref, n_ref, m_ref,
                       h_ref, co_ref, no_ref, mo_ref, *, heads, dqk, dv, gate_lane, k_scale):
    qk = qk_ref[0]
    q_rows = qk[0:heads, :]
    k_rows = qk[heads:2 * heads, :] * k_scale
    pad = jnp.zeros((dqk - 2 * heads, dqk), F32)
    cols = jnp.concatenate([q_rows, k_rows, pad], axis=0).T
    zm = zm_ref[0]
    ig = _softcap(zm[:, gate_lane:gate_lane + heads] + bi_ref[...])
    fg = _log_sigmoid(_softcap(zm[:, gate_lane + heads:gate_lane + 2 * heads] + bf_ref[...]))
    loga = fg + m_ref[0]
    m_t = jnp.maximum(loga, ig)
    dm = jnp.exp(ig - m_t)
    a = jnp.exp(loga - m_t)
    floor = jnp.exp(-m_t)
    mo_ref[0] = m_t
    v_all = v_ref[0]
    o_all = o_ref[0]
    n_all = n_ref[0]
    for h in range(heads):
        a_h, dm_h = a[:, h:h + 1], dm[:, h:h + 1]
        q_row, k_row = q_rows[h:h + 1, :], k_rows[h:h + 1, :]
        q_col, k_col = cols[:, h:h + 1], cols[:, heads + h:heads + h + 1]
        v_h, n_h = v_all[h:h + 1, :], n_all[h:h + 1, :]
        c_old = c_ref[0, h]
        s = jnp.sum(q_row * k_row, axis=-1, keepdims=True) * dm_h
        num = a_h * jnp.sum(c_old * q_col, axis=0, keepdims=True) + s * v_h
        den = a_h * jnp.sum(q_row * n_h, axis=-1, keepdims=True) + s
        hh = num / jnp.maximum(jnp.abs(den), floor[:, h:h + 1])
        hh = _rms(hh, gain_ref[h:h + 1, :]) * jax.nn.sigmoid(o_all[h:h + 1, :])
        h_ref[0, :, h * dv:(h + 1) * dv] = hh.astype(h_ref.dtype)
        co_ref[0, h] = a_h * c_old + (dm_h * k_col) * v_h
        no_ref[0, h:h + 1, :] = a_h * n_h + dm_h * k_row


def mlstm_sample(qk, v, o, zmisc, b_i, b_f, gain, state_c, state_n, state_m, layer, lay):
    heads, dqk, dv = lay["ml_heads"], lay["ml_dqk"], lay["ml_dv"]
    db = qk.shape[0]
    blocks = [((heads, dqk, dv), F32)] * 2 + [((2 * heads, dqk), F32), ((heads, dv), F32)]
    body = functools.partial(_mlstm_sample_body, heads=heads, dqk=dqk, dv=dv,
                             gate_lane=lay["misc_gate_lane"], k_scale=dqk ** -0.5)
    return pl.pallas_call(
        body,
        grid=(db,),
        in_specs=[pl.BlockSpec((1, heads), lambda b: (0, 0)),
                  pl.BlockSpec((1, heads), lambda b: (0, 0)),
                  pl.BlockSpec((1, 2 * heads, dqk), lambda b: (b, 0, 0)),
                  pl.BlockSpec((1, heads, dv), lambda b: (b, 0, 0)),
                  pl.BlockSpec((1, heads, dv), lambda b: (b, 0, 0)),
                  pl.BlockSpec((1, 1, LANES), lambda b: (b, 0, 0)),
                  pl.BlockSpec((heads, dv), lambda b: (0, 0)),
                  pl.BlockSpec((None, 1, heads, dqk, dv), lambda b: (layer, b, 0, 0, 0)),
                  pl.BlockSpec((None, 1, heads, dqk), lambda b: (layer, b, 0, 0)),
                  pl.BlockSpec((None, 1, 1, heads), lambda b: (layer, b, 0, 0))],
        out_specs=[pl.BlockSpec((1, 1, heads * dv), lambda b: (b, 0, 0)),
                   pl.BlockSpec((1, heads, dqk, dv), lambda b: (b, 0, 0, 0)),
                   pl.BlockSpec((1, heads, dqk), lambda b: (b, 0, 0)),
                   pl.BlockSpec((1, 1, heads), lambda b: (b, 0, 0))],
        out_shape=[jax.ShapeDtypeStruct((db, 1, heads * dv), BF16),
                   jax.ShapeDtypeStruct((db, heads, dqk, dv), F32),
                   jax.ShapeDtypeStruct((db, heads, dqk), F32),
                   jax.ShapeDtypeStruct((db, 1, heads), F32)],
        compiler_params=_params(("parallel",), blocks),
        name="mlstm_sample",
    )(b_i.reshape(1, heads), b_f.reshape(1, heads), qk, v, o, zmisc, gain,
      state_c, state_n, state_m.reshape(state_m.shape[0], db, 1, heads))


def _kv_prep_body(ckv_ref, zm_ref, g_ref, cos_ref, sin_ref, c32_ref, c16_ref, kr32_ref, kr16_ref, *, rope_dim):
    c = _rms(ckv_ref[...], g_ref[...])
    c32_ref[...] = c
    c16_ref[...] = c.astype(BF16)
    x = zm_ref[...]
    kr = (x * cos_ref[...] + _rope_rotate(x) * sin_ref[...])[:, :rope_dim]
    kr32_ref[...] = kr
    kr16_ref[...] = kr.astype(BF16)


def kv_prep(z, g, cos, sin, lay, *, rows, tm, table_blocks):
    kvw, rd = lay["kv_lora"], lay["rope_dim"]
    jc, jm = lay["ckv"] // kvw, lay["misc"] // LANES
    blocks = [((tm, kvw), F32)] * 2 + [((tm, LANES), F32)] * 5
    return pl.pallas_call(
        functools.partial(_kv_prep_body, rope_dim=rd),
        grid=(rows // tm,),
        in_specs=[pl.BlockSpec((tm, kvw), lambda i: (i, jc)),
                  pl.BlockSpec((tm, LANES), lambda i: (i, jm)),
                  pl.BlockSpec((1, kvw), lambda i: (0, 0)),
                  pl.BlockSpec((tm, LANES), lambda i: (i % table_blocks, 0)),
                  pl.BlockSpec((tm, LANES), lambda i: (i % table_blocks, 0))],
        out_specs=[pl.BlockSpec((tm, kvw), lambda i: (i, 0)),
                   pl.BlockSpec((tm, kvw), lambda i: (i, 0)),
                   pl.BlockSpec((tm, rd), lambda i: (i, 0)),
                   pl.BlockSpec((tm, rd), lambda i: (i, 0))],
        out_shape=[jax.ShapeDtypeStruct((rows, kvw), F32), jax.ShapeDtypeStruct((rows, kvw), BF16),
                   jax.ShapeDtypeStruct((rows, rd), F32), jax.ShapeDtypeStruct((rows, rd), BF16)],
        compiler_params=_params(("parallel",), blocks),
        name="mla_kv_prep",
    )(z, z, g.reshape(1, kvw), cos, sin)


def _q_prep_body(cq_ref, g_ref, wuq_ref, wuk_ref, cos_ref, sin_ref, qlat_ref, qrope_ref,
                 *, heads, nope, rope_dim, scale):
    hq = _rms(cq_ref[...], g_ref[...]).astype(BF16)
    q = _dot(hq, wuq_ref[...])
    for h in range(heads):
        qn = q[:, h * nope:(h + 1) * nope].astype(BF16)
        qlat_ref[h] = (_dot(qn, wuk_ref[h]) * scale).astype(qlat_ref.dtype)
    cos = cos_ref[...]
    sin = sin_ref[...]
    base = heads * nope
    per_tile = LANES // rope_dim
    for p in range(heads // per_tile):
        x = q[:, base + p * LANES: base + (p + 1) * LANES]
        r = ((x * cos + _rope_rotate(x) * sin) * scale).astype(qrope_ref.dtype)
        for t in range(per_tile):
            qrope_ref[p * per_tile + t] = r[:, t * rope_dim:(t + 1) * rope_dim]


def q_prep(z, g, wuq, wuk, cos, sin, lay, *, rows, tm, table_blocks):
    heads, nope, rd, kvw, ql = lay["mla_heads"], lay["nope"], lay["rope_dim"], lay["kv_lora"], lay["q_lora"]
    jq = lay["cq"] // ql
    blocks = [((tm, ql), F32), ((ql, heads * (nope + rd)), BF16), ((heads, nope, kvw), BF16),
              ((heads, tm, kvw), BF16), ((heads, tm, LANES), BF16)]
    return pl.pallas_call(
        functools.partial(_q_prep_body, heads=heads, nope=nope, rope_dim=rd, scale=(nope + rd) ** -0.5),
        grid=(rows // tm,),
        in_specs=[pl.BlockSpec((tm, ql), lambda i: (i, jq)),
                  pl.BlockSpec((1, ql), lambda i: (0, 0)),
                  pl.BlockSpec((ql, heads * (nope + rd)), lambda i: (0, 0)),
                  pl.BlockSpec((heads, nope, kvw), lambda i: (0, 0, 0)),
                  pl.BlockSpec((tm, LANES), lambda i: (i % table_blocks, 0)),
                  pl.BlockSpec((tm, LANES), lambda i: (i % table_blocks, 0))],
        out_specs=[pl.BlockSpec((heads, tm, kvw), lambda i: (0, i, 0)),
                   pl.BlockSpec((heads, tm, rd), lambda i: (0, i, 0))],
        out_shape=[jax.ShapeDtypeStruct((heads, rows, kvw), BF16),
                   jax.ShapeDtypeStruct((heads, rows, rd), BF16)],
        compiler_params=_params(("parallel",), blocks, _nbytes((tm, heads * (nope + rd)), F32)),
        name="mla_q_prep",
    )(z, g.reshape(1, ql), wuq, wuk, cos, sin)


def _mla_prompt_body(ql_ref, qr_ref, kc_ref, kr_ref, wuv_ref, o_ref, m_scr, l_scr, acc_scr,
                     *, heads, tq, tk, vdim):
    i = pl.program_id(1)
    kvw = ql_ref.shape[-1]
    q1 = ql_ref[...].reshape(heads * tq, kvw)
    q2 = qr_ref[...].reshape(heads * tq, qr_ref.shape[-1])
    m_scr[...] = jnp.full_like(m_scr, -jnp.inf)
    l_scr[...] = jnp.zeros_like(l_scr)
    acc_scr[...] = jnp.zeros_like(acc_scr)

    def update(start, masked):
        kc = kc_ref[pl.ds(start, tk), :]
        kr = kr_ref[pl.ds(start, tk), :]
        s = _dot_nt(q1, kc) + _dot_nt(q2, kr)
        if masked:
            q_pos = i * tq + lax.broadcasted_iota(jnp.int32, (tq, tk), 0)
            k_pos = start + lax.broadcasted_iota(jnp.int32, (tq, tk), 1)
            s = jnp.where((k_pos <= q_pos)[None], s.reshape(heads, tq, tk), -jnp.inf).reshape(heads * tq, tk)
        m_old = m_scr[...]
        m_new = jnp.maximum(m_old, jnp.max(s, axis=-1, keepdims=True))
        alpha = jnp.exp(m_old - m_new)
        p = jnp.exp(s - m_new)
        l_scr[...] = l_scr[...] * alpha + jnp.sum(p, axis=-1, keepdims=True)
        acc_scr[...] = acc_scr[...] * alpha + _dot(p.astype(BF16), kc)
        m_scr[...] = m_new

    n_full = (i * tq) // tk

    def full_step(j, carry):
        update(pl.multiple_of(j * tk, tk), False)
        return carry

    lax.fori_loop(0, n_full, full_step, 0)
    update(pl.multiple_of(n_full * tk, tk), True)
    o = acc_scr[...] / l_scr[...]
    for h in range(heads):
        oh = o[h * tq:(h + 1) * tq, :].astype(BF16)
        o_ref[:, h * vdim:(h + 1) * vdim] = _dot(oh, wuv_ref[h]).astype(o_ref.dtype)


def mla_prompt(qlat, qrope, kc, kr, wuv, lay, *, batch, seq, tq, tk):
    heads, kvw, rd, vdim = lay["mla_heads"], lay["kv_lora"], lay["rope_dim"], lay["v_dim"]
    nq = seq // tq
    assert tk % tq == 0 and seq % tk == 0
    blocks = [((heads, tq, kvw), BF16), ((heads, tq, LANES), BF16), ((seq, kvw), BF16), ((seq, LANES), BF16),
              ((heads, kvw, vdim), BF16), ((tq, heads * vdim), BF16)]
    scratch = _nbytes((heads * tq, kvw), F32) + 2 * _nbytes((heads * tq, LANES), F32)
    body = functools.partial(_mla_prompt_body, heads=heads, tq=tq, tk=tk, vdim=vdim)
    return pl.pallas_call(
        body,
        grid=(batch, nq),
        in_specs=[pl.BlockSpec((heads, tq, kvw), lambda b, i: (0, b * nq + i, 0)),
                  pl.BlockSpec((heads, tq, rd), lambda b, i: (0, b * nq + i, 0)),
                  pl.BlockSpec((seq, kvw), lambda b, i: (b, 0)),
                  pl.BlockSpec((seq, rd), lambda b, i: (b, 0)),
                  pl.BlockSpec((heads, kvw, vdim), lambda b, i: (0, 0, 0))],
        out_specs=pl.BlockSpec((tq, heads * vdim), lambda b, i: (b * nq + i, 0)),
        out_shape=jax.ShapeDtypeStruct((batch * seq, heads * vdim), BF16),
        scratch_shapes=[pltpu.VMEM((heads * tq, 1), F32), pltpu.VMEM((heads * tq, 1), F32),
                        pltpu.VMEM((heads * tq, kvw), F32)],
        compiler_params=_params(("parallel", "arbitrary"), blocks,
                                scratch + 4 * _nbytes((heads * tq, tk), F32)),
        name="mla_prompt",
    )(qlat, qrope, kc, kr, wuv)


def _mla_sample_body(pt_ref, ql_ref, qr_ref, cn_ref, krn_ref, lat_hbm, kr_hbm, o_ref,
                     lat_buf, kr_buf, sem, m_scr, l_scr, acc_scr, *, layer, pages):
    page = lat_buf.shape[2]
    c = pl.program_id(1)
    n_chunks = pl.num_programs(1)
    step = pl.program_id(0) * n_chunks + c
    total = pl.num_programs(0) * n_chunks
    slot = step % 2

    def chunk_copies(chunk_step, sl):
        copies = []
        for j in range(pages):
            pg = pt_ref[chunk_step * pages + j]
            copies.append(pltpu.make_async_copy(lat_hbm.at[layer, pg], lat_buf.at[sl, j], sem.at[0, sl]))
            copies.append(pltpu.make_async_copy(kr_hbm.at[layer, pg], kr_buf.at[sl, :, pl.ds(j * page, page)],
                                                sem.at[1, sl]))
        return copies

    @pl.when(step == 0)
    def _():
        for cp in chunk_copies(0, 0):
            cp.start()

    @pl.when(step + 1 < total)
    def _():
        for cp in chunk_copies(step + 1, 1 - slot):
            cp.start()

    ql = ql_ref[0]
    qr = qr_ref[0]

    @pl.when(c == 0)
    def _():
        cn = cn_ref[0].astype(F32)
        krn = krn_ref[0].astype(F32)
        m_scr[...] = (jnp.sum(ql.astype(F32) * cn, axis=-1, keepdims=True)
                      + jnp.sum(qr.astype(F32) * krn, axis=-1, keepdims=True))
        l_scr[...] = jnp.ones_like(l_scr)
        acc_scr[...] = jnp.broadcast_to(cn, acc_scr.shape)

    for cp in chunk_copies(step, slot):
        cp.wait()

    keys = lat_buf[slot].reshape(pages * page, lat_buf.shape[-1]).astype(BF16)
    s = _dot_nt(ql, keys) + _dot(qr, kr_buf[slot].astype(BF16))
    m_old = m_scr[...]
    m_new = jnp.maximum(m_old, jnp.max(s, axis=-1, keepdims=True))
    alpha = jnp.exp(m_old - m_new)
    p = jnp.exp(s - m_new)
    l_new = l_scr[...] * alpha + jnp.sum(p, axis=-1, keepdims=True)
    acc = acc_scr[...] * alpha + _dot(p.astype(BF16), keys)
    m_scr[...] = m_new
    l_scr[...] = l_new
    acc_scr[...] = acc

    @pl.when(c == n_chunks - 1)
    def _():
        o_ref[0] = (acc / l_new).astype(o_ref.dtype)


def mla_sample(qlat, qrope, c_new, kr_new, cache_lat, cache_kr_t, page_table, layer, lay, *, pages):
    db, heads, kvw = qlat.shape
    rd = qrope.shape[-1]
    n_pages = page_table.shape[1]
    page = cache_lat.shape[2]
    assert n_pages % pages == 0
    buffers = _nbytes((2, pages, page, kvw), F32) + _nbytes((2, pages, rd, page), F32)
    blocks = [((heads, kvw), BF16), ((heads, LANES), BF16), ((heads, kvw), F32)]
    body = functools.partial(_mla_sample_body, layer=layer, pages=pages)
    hbm = pl.BlockSpec(memory_space=pl.ANY)
    grid_spec = pltpu.PrefetchScalarGridSpec(
        num_scalar_prefetch=1,
        grid=(db, n_pages // pages),
        in_specs=[pl.BlockSpec((1, heads, kvw), lambda b, c, pt: (b, 0, 0)),
                  pl.BlockSpec((1, heads, rd), lambda b, c, pt: (b, 0, 0)),
                  pl.BlockSpec((1, 1, kvw), lambda b, c, pt: (b, 0, 0)),
                  pl.BlockSpec((1, 1, rd), lambda b, c, pt: (b, 0, 0)),
                  hbm, hbm],
        out_specs=pl.BlockSpec((1, heads, kvw), lambda b, c, pt: (b, 0, 0)),
        scratch_shapes=[pltpu.VMEM((2, pages, page, kvw), F32), pltpu.VMEM((2, rd, pages * page), F32),
                        pltpu.SemaphoreType.DMA((2, 2)),
                        pltpu.VMEM((heads, 1), F32), pltpu.VMEM((heads, 1), F32),
                        pltpu.VMEM((heads, kvw), F32)])
    return pl.pallas_call(
        body,
        grid_spec=grid_spec,
        out_shape=jax.ShapeDtypeStruct((db, heads, kvw), BF16),
        compiler_params=_params(("arbitrary", "arbitrary"), blocks, buffers),
        name="mla_sample",
    )(page_table.reshape(-1), qlat, qrope, c_new, kr_new, cache_lat, cache_kr_t)


def _uv_body(o_ref, w_ref, h_ref):
    h_ref[...] = _dot(o_ref[0], w_ref[0]).astype(h_ref.dtype)


def mla_up_v(o_lat, wuv):
    heads, m, kvw = o_lat.shape
    vdim = wuv.shape[-1]
    blocks = [((m, kvw), BF16), ((kvw, vdim), BF16), ((m, vdim), BF16)]
    return pl.pallas_call(
        _uv_body,
        grid=(heads,),
        in_specs=[pl.BlockSpec((1, m, kvw), lambda h: (h, 0, 0)),
                  pl.BlockSpec((1, kvw, vdim), lambda h: (h, 0, 0))],
        out_specs=pl.BlockSpec((m, vdim), lambda h: (0, h)),
        out_shape=jax.ShapeDtypeStruct((m, heads * vdim), BF16),
        compiler_params=_params(("parallel",), blocks),
        name="mla_up_v",
    )(o_lat, wuv)


def _mem_attend_prompt_body(q_ref, k_ref, v_ref, o_ref, *, heads, dim):
    q = q_ref[...]
    k = k_ref[...].astype(BF16)
    v = v_ref[...].astype(BF16)
    scale = dim ** -0.5
    for h in range(heads):
        sl = slice(h * dim, (h + 1) * dim)
        s = _dot_nt(q[:, sl], k[:, sl]) * scale
        p = jnp.exp(s - jnp.max(s, axis=-1, keepdims=True))
        p = p / jnp.sum(p, axis=-1, keepdims=True)
        o_ref[:, sl] = _dot(p.astype(BF16), v[:, sl]).astype(o_ref.dtype)


def mem_attend_prompt(q, kv, lay, *, batch, seq, tq):
    heads, dim, n_mem = lay["mem_heads"], lay["mem_dim"], lay["n_mem"]
    w = heads * dim
    nq = seq // tq
    blocks = [((tq, w), BF16), ((n_mem, w), F32), ((n_mem, w), F32), ((tq, w), BF16)]
    body = functools.partial(_mem_attend_prompt_body, heads=heads, dim=dim)
    return pl.pallas_call(
        body,
        grid=(batch, nq),
        in_specs=[pl.BlockSpec((tq, w), lambda b, i: (b * nq + i, 0)),
                  pl.BlockSpec((n_mem, w), lambda b, i: (b, 0)),
                  pl.BlockSpec((n_mem, w), lambda b, i: (b, 1))],
        out_specs=pl.BlockSpec((tq, w), lambda b, i: (b * nq + i, 0)),
        out_shape=jax.ShapeDtypeStruct((batch * seq, w), BF16),
        compiler_params=_params(("parallel", "parallel"), blocks, 4 * _nbytes((tq, n_mem), F32)),
        name="mem_attend_prompt",
    )(q, kv, kv)


def _mem_attend_sample_body(q_ref, k_ref, v_ref, o_ref, *, group, heads, dim):
    rows = q_ref.shape[1]
    cols = k_ref.shape[1]
    col_head = lax.broadcasted_iota(jnp.int32, (rows, cols), 1) % heads
    row_head = lax.broadcasted_iota(jnp.int32, (rows, cols), 0) % heads
    own_head = col_head == row_head
    scale = dim ** -0.5
    for g in range(group):
        k = k_ref[g].astype(BF16)
        v = v_ref[g].astype(BF16)
        s = jnp.where(own_head, _dot_nt(q_ref[g], k) * scale, -jnp.inf)
        p = jnp.exp(s - jnp.max(s, axis=-1, keepdims=True))
        p = p / jnp.sum(p, axis=-1, keepdims=True)
        o_ref[g] = _dot(p.astype(BF16), v).astype(o_ref.dtype)


def mem_attend_sample(q, cache_k, cache_v, layer, lay, *, group):
    heads, dim = lay["mem_heads"], lay["mem_dim"]
    db, rows, _ = q.shape
    cols = cache_k.shape[2]
    blocks = [((group, cols, dim), F32)] * 2 + [((group, rows, dim), BF16)] * 2
    body = functools.partial(_mem_attend_sample_body, group=group, heads=heads, dim=dim)
    return pl.pallas_call(
        body,
        grid=(db // group,),
        in_specs=[pl.BlockSpec((group, rows, dim), lambda b: (b, 0, 0)),
                  pl.BlockSpec((None, group, cols, dim), lambda b: (layer, b, 0, 0)),
                  pl.BlockSpec((None, group, cols, dim), lambda b: (layer, b, 0, 0))],
        out_specs=pl.BlockSpec((group, rows, dim), lambda b: (b, 0, 0)),
        out_shape=jax.ShapeDtypeStruct((db, rows, dim), BF16),
        compiler_params=_params(("parallel",), blocks),
        name="mem_attend_sample",
    )(q, cache_k, cache_v)


def _layout(shapes):
    lay = dict(shapes)
    qk_w = lay["ml_heads"] * lay["ml_dqk"]
    v_w = lay["ml_heads"] * lay["ml_dv"]
    d = lay["d_model"]
    off = 0
    for name, width in (("q", qk_w), ("k", qk_w), ("v", v_w), ("o", v_w), ("cq", lay["q_lora"]),
                        ("ga", d), ("gb", d), ("ckv", lay["kv_lora"]), ("misc", LANES)):
        lay[name] = off
        off += width
    lay["z_cols"] = off
    lay["misc_gate_lane"] = lay["rope_dim"]
    assert lay["rope_dim"] + 2 * lay["ml_heads"] <= LANES
    return lay


def _repack_w_in(w, lay):
    heads = lay["ml_heads"]
    qk_w = heads * lay["ml_dqk"]
    v_w = heads * lay["ml_dv"]
    d = lay["d_model"]
    o = 0
    src = {}
    for name, width in (("q", qk_w), ("k", qk_w), ("v", v_w), ("i", heads), ("f", heads), ("o", v_w),
                        ("cq", lay["q_lora"]), ("ckv", lay["kv_lora"]), ("kr", lay["rope_dim"]),
                        ("ga", d), ("gb", d)):
        src[name] = (o, o + width)
        o += width
    pad = LANES - lay["rope_dim"] - 2 * heads
    parts = [w[:, src[n][0]:src[n][1]] for n in ("q", "k", "v", "o", "cq", "ga", "gb", "ckv", "kr", "i", "f")]
    parts.append(jnp.zeros((w.shape[0], pad), w.dtype))
    return jnp.concatenate(parts, axis=1).astype(BF16)


def _repack_w_uq(w, lay):
    heads, nope, rd = lay["mla_heads"], lay["nope"], lay["rope_dim"]
    w3 = w.reshape(w.shape[0], heads, nope + rd)
    return jnp.concatenate([w3[:, :, :nope].reshape(w.shape[0], heads * nope),
                            w3[:, :, nope:].reshape(w.shape[0], heads * rd)], axis=1).astype(BF16)


def _rope_tables(pos, rope_dim):
    half = rope_dim // 2
    inv = ROPE_THETA ** (-jnp.arange(half, dtype=F32) / half)
    ang = pos.astype(F32)[:, None] * inv[None, :]
    reps = LANES // half
    return jnp.tile(jnp.cos(ang), (1, reps)), jnp.tile(jnp.sin(ang), (1, reps))


def _tile(n, want):
    if n <= want:
        return n
    t = want - want % BF16_ROWS
    while n % t:
        t -= BF16_ROWS
    return t


def kernel(x_prompt, mem_prompt, x_sample, state_mlstm_C, state_mlstm_n, state_mlstm_m, cache_mla_latent, cache_mla_krope, cache_mem_k, cache_mem_v, page_table, norm_ffn1, ffn1_w_gate, ffn1_w_up, ffn1_w_down, norm_mix, w_in, mlstm_b_i, mlstm_b_f, mlstm_norm, mla_norm_q, mla_w_uq, mla_norm_kv, mla_w_uk, mla_w_uv, w_branch_mlstm, w_branch_mla, w_out, norm_mem, norm_mem_src, mem_w_q, mem_w_k, mem_w_v, mem_w_o, norm_ffn2, ffn2_w_gate, ffn2_w_up, ffn2_w_down, norm_final):
    batch, seq, d_model = x_prompt.shape
    db, dec_seq, _ = x_sample.shape
    assert dec_seq == 1
    depth = norm_ffn1.shape[0]
    n_mem = mem_prompt.shape[1]
    lay = _layout(dict(
        d_model=d_model, ml_heads=state_mlstm_C.shape[2], ml_dqk=state_mlstm_C.shape[3],
        ml_dv=state_mlstm_C.shape[4], mla_heads=mla_w_uk.shape[2], nope=mla_w_uk.shape[3],
        q_lora=mla_w_uq.shape[1], kv_lora=mla_w_uk.shape[1], rope_dim=cache_mla_krope.shape[3],
        v_dim=mla_w_uv.shape[3], mem_heads=cache_mem_k.shape[3], mem_dim=cache_mem_k.shape[4], n_mem=n_mem))
    heads, dqk, dv = lay["ml_heads"], lay["ml_dqk"], lay["ml_dv"]
    mem_heads, mem_dim = lay["mem_heads"], lay["mem_dim"]
    mem_w = mem_heads * mem_dim
    rd, kvw = lay["rope_dim"], lay["kv_lora"]
    mp = batch * seq
    m_all = mp + db

    tm = _tile(m_all, 1024)
    tr = _tile(m_all, 320)
    tn = 256
    tn_in = 640 if lay["z_cols"] % 640 == 0 else LANES
    chunk = min(256, seq)
    tm_prep = _tile(seq, 256)
    tk_mla = min(512, seq)
    tq_mla = min(128, seq)

    past_len = page_table.shape[1] * cache_mla_latent.shape[2]
    cos_p, sin_p = _rope_tables(jnp.arange(seq), rd)
    cos_s, sin_s = _rope_tables(jnp.full((db,), past_len), rd)

    cache_k4 = cache_mem_k.reshape(depth, db, n_mem * mem_heads, mem_dim)
    cache_v4 = cache_mem_v.reshape(depth, db, n_mem * mem_heads, mem_dim)
    cache_kr_t = jnp.swapaxes(cache_mla_krope, 2, 3)
    mem_src = mem_prompt.reshape(batch * n_mem, d_model)

    x = jnp.concatenate([x_prompt.reshape(mp, d_model), x_sample.reshape(db, d_model)], axis=0)
    out_p = [[] for _ in range(7)]
    out_s = [[] for _ in range(5)]
    for l in range(depth):
        w_in_l = _repack_w_in(w_in[l], lay)
        wuq = _repack_w_uq(mla_w_uq[l], lay)
        wuk = jnp.transpose(mla_w_uk[l], (1, 2, 0)).astype(BF16)
        wuv = jnp.transpose(mla_w_uv[l], (1, 0, 2)).astype(BF16)
        w_mem_kv = jnp.concatenate([mem_w_k[l], mem_w_v[l]], axis=1)

        h = rmsnorm(x, norm_ffn1[l], BF16, tm=tr)
        u = swiglu_up(h, ffn1_w_gate, ffn1_w_up, tm=tm, tn=tn, layer=l)
        x = matmul_residual(u, ffn1_w_down, x, 0.5, tm=tm, tn=tn, layer=l)
        h = rmsnorm(x, norm_mix[l], BF16, tm=tr)
        z = matmul(h, w_in_l, F32, tm=tm, tn=tn_in)
        zs = z[mp:]

        hm_src = rmsnorm(mem_src, norm_mem_src[l], BF16, tm=_tile(batch * n_mem, 256))
        mem_kv = matmul(hm_src, w_mem_kv, F32, tm=_tile(batch * n_mem, 1024), tn=tn)
        gate_cols = z[:mp, lay["misc"] + rd: lay["misc"] + rd + 2 * heads]
        gates_row = gate_cols.reshape(batch, seq, 2, heads).transpose(0, 3, 2, 1)
        hm_p, p_c, p_n, p_m = mlstm_prompt(z, gates_row, mlstm_b_i[l], mlstm_b_f[l], mlstm_norm[l], lay,
                                           batch=batch, seq=seq, chunk=chunk)
        c32, c16, kr32, kr16 = kv_prep(z, mla_norm_kv[l], cos_p, sin_p, lay, rows=mp, tm=tm_prep,
                                       table_blocks=seq // tm_prep)
        qlat, qrope = q_prep(z, mla_norm_q[l], wuq, wuk, cos_p, sin_p, lay, rows=mp, tm=tm_prep,
                             table_blocks=seq // tm_prep)
        ha_p = mla_prompt(qlat, qrope, c16, kr16, wuv, lay, batch=batch, seq=seq, tq=tq_mla, tk=tk_mla)
        for lst, a in zip(out_p, (p_c, p_n.reshape(batch, heads, dqk), p_m.reshape(batch, heads),
                                  c32.reshape(batch, seq, kvw), kr32.reshape(batch, seq, rd),
                                  mem_kv[:, :mem_w].reshape(batch, n_mem, mem_heads, mem_dim),
                                  mem_kv[:, mem_w:].reshape(batch, n_mem, mem_heads, mem_dim))):
            lst.append(a)

        qk_s = zs[:, lay["q"]:lay["v"]].reshape(db, 2 * heads, dqk)
        v_s = zs[:, lay["v"]:lay["o"]].reshape(db, heads, dv)
        o_s = zs[:, lay["o"]:lay["cq"]].reshape(db, heads, dv)
        misc_s = zs[:, lay["misc"]:].reshape(db, 1, LANES)
        hm_s, s_c, s_n, s_m = mlstm_sample(qk_s, v_s, o_s, misc_s, mlstm_b_i[l], mlstm_b_f[l], mlstm_norm[l],
                                           state_mlstm_C, state_mlstm_n, state_mlstm_m, l, lay)
        c32, c16, kr32, kr16 = kv_prep(zs, mla_norm_kv[l], cos_s, sin_s, lay, rows=db, tm=db, table_blocks=1)
        qlat, qrope = q_prep(zs, mla_norm_q[l], wuq, wuk, cos_s, sin_s, lay, rows=db, tm=db, table_blocks=1)
        o_lat = mla_sample(qlat.transpose(1, 0, 2), qrope.transpose(1, 0, 2), c16.reshape(db, 1, kvw),
                           kr16.reshape(db, 1, rd), cache_mla_latent, cache_kr_t, page_table, l, lay,
                           pages=min(32, page_table.shape[1]))
        ha_s = mla_up_v(o_lat.transpose(1, 0, 2), wuv)
        for lst, a in zip(out_s, (s_c, s_n, s_m.reshape(db, heads), c32.reshape(db, 1, kvw), kr32.reshape(db, 1, rd))):
            lst.append(a)

        h_m = jnp.concatenate([hm_p, hm_s.reshape(db, heads * dv)], axis=0)
        h_a = jnp.concatenate([ha_p, ha_s], axis=0)
        merged = gated_merge(h_m, h_a, w_branch_mlstm, w_branch_mla, z, lay["ga"], lay["gb"], tm=tm, tn=tn, layer=l)
        x = matmul_residual(merged, w_out, x, 1.0, tm=tm, tn=tn, layer=l)
        h = rmsnorm(x, norm_mem[l], BF16, tm=tr)
        q = matmul(h, mem_w_q, BF16, tm=tm, tn=tn, layer=l)
        o_p = mem_attend_prompt(q, mem_kv, lay, batch=batch, seq=seq, tq=_tile(seq, 512))
        q_s = jnp.pad(q[mp:].reshape(db, mem_heads, mem_dim), ((0, 0), (0, BF16_ROWS - mem_heads), (0, 0)))
        o_s = mem_attend_sample(q_s, cache_k4, cache_v4, l, lay, group=math.gcd(db, 4))
        o = jnp.concatenate([o_p, o_s[:, :mem_heads].reshape(db, mem_w)], axis=0)
        x = matmul_residual(o, mem_w_o, x, 1.0, tm=tm, tn=tn, layer=l)
        h = rmsnorm(x, norm_ffn2[l], BF16, tm=tr)
        u = swiglu_up(h, ffn2_w_gate, ffn2_w_up, tm=tm, tn=tn, layer=l)
        x = matmul_residual(u, ffn2_w_down, x, 0.5, tm=tm, tn=tn, layer=l)

    y_prompt = rmsnorm(x, norm_final, F32, tm=_tile(mp, 256), rows=mp).reshape(batch, seq, d_model)
    y_sample = rmsnorm(x, norm_final, F32, tm=math.gcd(mp, db), row0=mp, rows=db).reshape(db, 1, d_model)
    return (y_prompt, y_sample) + tuple(jnp.stack(a) for a in out_p) + tuple(jnp.stack(a) for a in out_s)
```

```python
import functools
import math

import jax
import jax.numpy as jnp
from jax import lax
from jax.experimental import pallas as pl
from jax.experimental.pallas import tpu as pltpu

F32 = jnp.float32
BF16 = jnp.bfloat16

EPS = 1e-6
GATE_CAP = 15.0
ROPE_THETA = 10000.0

LANES = 128
BF16_ROWS = 16
VMEM_BYTES = 64 * 1024 * 1024
VMEM_TEMP_BYTES = 12 * 1024 * 1024


def _nbytes(shape, dtype):
    return math.prod(shape) * jnp.dtype(dtype).itemsize


def _params(semantics, blocks, scratch_bytes=0):
    need = 2 * sum(_nbytes(s, d) for s, d in blocks) + scratch_bytes + VMEM_TEMP_BYTES
    return pltpu.CompilerParams(dimension_semantics=semantics,
                                vmem_limit_bytes=min(need, VMEM_BYTES - 4 * 1024 * 1024))


def _dot(a, b):
    return jnp.dot(a, b, preferred_element_type=F32)


def _dot_nt(a, b):
    return lax.dot_general(a, b, (((1,), (1,)), ((), ())), preferred_element_type=F32)


def _dot_tn(a, b):
    return lax.dot_general(a, b, (((0,), (0,)), ((), ())), preferred_element_type=F32)


def _softcap(a):
    return GATE_CAP * jnp.tanh(a / GATE_CAP)


def _log_sigmoid(a):
    return jnp.minimum(a, 0.0) - jnp.log1p(jnp.exp(-jnp.abs(a)))


def _rms(x, g):
    return x * lax.rsqrt(jnp.mean(x * x, axis=-1, keepdims=True) + EPS) * g


def _rope_rotate(x):
    lane = lax.broadcasted_iota(jnp.int32, x.shape, x.ndim - 1)
    first_half = (lane % 64) < 32
    return jnp.where(first_half, -pltpu.roll(x, 96, x.ndim - 1), pltpu.roll(x, 32, x.ndim - 1))


def _rmsnorm_body(x_ref, g_ref, o_ref):
    o_ref[...] = _rms(x_ref[...].astype(F32), g_ref[...].astype(F32)).astype(o_ref.dtype)


def rmsnorm(x, g, out_dtype, *, tm, row0=0, rows=None):
    w = x.shape[1]
    rows = x.shape[0] if rows is None else rows
    i0 = row0 // tm
    assert i0 * tm == row0 and rows % tm == 0
    blocks = [((tm, w), x.dtype), ((tm, w), out_dtype)]
    return pl.pallas_call(
        _rmsnorm_body,
        grid=(rows // tm,),
        in_specs=[pl.BlockSpec((tm, w), lambda i: (i0 + i, 0)),
                  pl.BlockSpec((1, w), lambda i: (0, 0))],
        out_specs=pl.BlockSpec((tm, w), lambda i: (i, 0)),
        out_shape=jax.ShapeDtypeStruct((rows, w), out_dtype),
        compiler_params=_params(("parallel",), blocks),
        name="rmsnorm",
    )(x, g.reshape(1, w))


def _weight_spec(w, layer, tn):
    k = w.shape[-2]
    if w.ndim == 3:
        return pl.BlockSpec((None, k, tn), lambda j, i: (layer, 0, j))
    return pl.BlockSpec((k, tn), lambda j, i: (0, j))


def _weight_scratch(w, tn):
    return [] if w.dtype == BF16 else [pltpu.VMEM((w.shape[-2], tn), BF16)]


def _weight_bytes(w, tn):
    k = w.shape[-2]
    return 2 * _nbytes((k, tn), w.dtype) + (0 if w.dtype == BF16 else _nbytes((k, tn), BF16))


def _resident_bf16(w_ref, w16_ref):
    if w16_ref is None:
        return w_ref[...]

    @pl.when(pl.program_id(1) == 0)
    def _():
        w16_ref[...] = w_ref[...].astype(BF16)

    return w16_ref[...]


def _mm_body(a_ref, w_ref, o_ref, *scratch):
    w = _resident_bf16(w_ref, scratch[0] if scratch else None)
    o_ref[...] = _dot(a_ref[...], w).astype(o_ref.dtype)


def matmul(a, w, out_dtype, *, tm, tn, layer=0):
    m, k = a.shape
    n = w.shape[-1]
    vmem = 2 * _nbytes((tm, k), a.dtype) + _weight_bytes(w, tn) + 3 * _nbytes((tm, tn), F32)
    return pl.pallas_call(
        _mm_body,
        grid=(pl.cdiv(n, tn), m // tm),
        in_specs=[pl.BlockSpec((tm, k), lambda j, i: (i, 0)), _weight_spec(w, layer, tn)],
        out_specs=pl.BlockSpec((tm, tn), lambda j, i: (i, j)),
        out_shape=jax.ShapeDtypeStruct((m, n), out_dtype),
        scratch_shapes=_weight_scratch(w, tn),
        compiler_params=_params(("parallel", "arbitrary"), [], vmem),
        name="matmul",
    )(a, w)


def _mm_residual_body(a_ref, w_ref, r_ref, o_ref, *scratch, scale):
    w = _resident_bf16(w_ref, scratch[0] if scratch else None)
    o_ref[...] = r_ref[...] + scale * _dot(a_ref[...], w)


def matmul_residual(a, w, res, scale, *, tm, tn, layer=0):
    m, k = a.shape
    n = w.shape[-1]
    vmem = 2 * _nbytes((tm, k), a.dtype) + _weight_bytes(w, tn) + 5 * _nbytes((tm, tn), F32)
    return pl.pallas_call(
        functools.partial(_mm_residual_body, scale=scale),
        grid=(n // tn, m // tm),
        in_specs=[pl.BlockSpec((tm, k), lambda j, i: (i, 0)), _weight_spec(w, layer, tn),
                  pl.BlockSpec((tm, tn), lambda j, i: (i, j))],
        out_specs=pl.BlockSpec((tm, tn), lambda j, i: (i, j)),
        out_shape=jax.ShapeDtypeStruct((m, n), F32),
        scratch_shapes=_weight_scratch(w, tn),
        compiler_params=_params(("parallel", "arbitrary"), [], vmem),
        name="matmul_residual",
    )(a, w, res)


def _swiglu_body(a_ref, wg_ref, wu_ref, o_ref, wg16_ref=None, wu16_ref=None):
    a = a_ref[...]
    g = _dot(a, _resident_bf16(wg_ref, wg16_ref))
    u = _dot(a, _resident_bf16(wu_ref, wu16_ref))
    o_ref[...] = (g * jax.nn.sigmoid(g) * u).astype(o_ref.dtype)


def swiglu_up(a, wg, wu, *, tm, tn, layer=0):
    m, k = a.shape
    n = wg.shape[-1]
    vmem = 2 * _nbytes((tm, k), a.dtype) + 2 * _weight_bytes(wg, tn) + 4 * _nbytes((tm, tn), F32)
    return pl.pallas_call(
        _swiglu_body,
        grid=(pl.cdiv(n, tn), m // tm),
        in_specs=[pl.BlockSpec((tm, k), lambda j, i: (i, 0)),
                  _weight_spec(wg, layer, tn), _weight_spec(wu, layer, tn)],
        out_specs=pl.BlockSpec((tm, tn), lambda j, i: (i, j)),
        out_shape=jax.ShapeDtypeStruct((m, n), BF16),
        scratch_shapes=_weight_scratch(wg, tn) + _weight_scratch(wu, tn),
        compiler_params=_params(("parallel", "arbitrary"), [], vmem),
        name="swiglu_up",
    )(a, wg, wu)


def _merge_body(hm_ref, ha_ref, wa_ref, wb_ref, ga_ref, gb_ref, o_ref, wa16_ref=None, wb16_ref=None):
    ya = _dot(hm_ref[...], _resident_bf16(wa_ref, wa16_ref))
    yb = _dot(ha_ref[...], _resident_bf16(wb_ref, wb16_ref))
    o_ref[...] = (jax.nn.sigmoid(ga_ref[...]) * ya + jax.nn.sigmoid(gb_ref[...]) * yb).astype(o_ref.dtype)


def gated_merge(hm, ha, wa, wb, z, ga_col, gb_col, *, tm, tn, layer):
    m, ka = hm.shape
    kb = ha.shape[1]
    n = wa.shape[-1]
    ja, jb = ga_col // tn, gb_col // tn
    assert ja * tn == ga_col and jb * tn == gb_col
    vmem = (2 * _nbytes((tm, ka + kb), BF16) + _weight_bytes(wa, tn) + _weight_bytes(wb, tn)
            + 8 * _nbytes((tm, tn), F32))
    return pl.pallas_call(
        _merge_body,
        grid=(n // tn, m // tm),
        in_specs=[pl.BlockSpec((tm, ka), lambda j, i: (i, 0)),
                  pl.BlockSpec((tm, kb), lambda j, i: (i, 0)),
                  _weight_spec(wa, layer, tn), _weight_spec(wb, layer, tn),
                  pl.BlockSpec((tm, tn), lambda j, i: (i, ja + j)),
                  pl.BlockSpec((tm, tn), lambda j, i: (i, jb + j))],
        out_specs=pl.BlockSpec((tm, tn), lambda j, i: (i, j)),
        out_shape=jax.ShapeDtypeStruct((m, n), BF16),
        scratch_shapes=_weight_scratch(wa, tn) + _weight_scratch(wb, tn),
        compiler_params=_params(("parallel", "arbitrary"), [], vmem),
        name="gated_merge",
    )(hm, ha, wa, wb, z, z)


def _mlstm_prompt_body(bi_ref, bf_ref, q_ref, k_ref, v_ref, o_ref, zm_ref, grow_ref, gain_ref,
                       h_ref, c_ref, n_ref, m_ref, *, chunk, heads, gate_lane, k_scale):
    hd = pl.program_id(1)

    @pl.when(pl.program_id(2) == 0)
    def _():
        c_ref[...] = jnp.zeros_like(c_ref)
        n_ref[...] = jnp.zeros_like(n_ref)
        m_ref[...] = jnp.zeros_like(m_ref)

    b_i = bi_ref[hd]
    b_f = bf_ref[hd]
    zm = zm_ref[...]
    lane = lax.broadcasted_iota(jnp.int32, zm.shape, 1)
    i_col = jnp.sum(jnp.where(lane == gate_lane + hd, zm, 0.0), axis=-1, keepdims=True)
    f_col = jnp.sum(jnp.where(lane == gate_lane + heads + hd, zm, 0.0), axis=-1, keepdims=True)
    grow = grow_ref[0, 0]
    ig_col = _softcap(i_col + b_i)
    fg_col = _log_sigmoid(_softcap(f_col + b_f))
    ig_row = _softcap(grow[0:1, :] + b_i)
    fg_row = _log_sigmoid(_softcap(grow[1:2, :] + b_f))

    t_idx = lax.broadcasted_iota(jnp.int32, (chunk, chunk), 0)
    s_idx = lax.broadcasted_iota(jnp.int32, (chunk, chunk), 1)
    causal = s_idx <= t_idx
    b_col = jnp.sum(jnp.where(causal, fg_row, 0.0), axis=-1, keepdims=True)
    b_row = jnp.sum(jnp.where(t_idx <= s_idx, fg_col, 0.0), axis=0, keepdims=True)

    m_prev = m_ref[0, 0]
    logd = jnp.where(causal, b_col - b_row + ig_row, -jnp.inf)
    loga = b_col + m_prev
    m_t = jnp.maximum(loga, jnp.max(logd, axis=-1, keepdims=True))
    dmat = jnp.exp(logd - m_t)
    a = jnp.exp(loga - m_t)

    q = q_ref[...]
    k = k_ref[...] * k_scale
    v = v_ref[...]
    qb, kb, vb = q.astype(BF16), k.astype(BF16), v.astype(BF16)
    c_old = c_ref[0, 0]
    n_old = n_ref[0, 0]
    s = _dot_nt(qb, kb) * dmat
    num = a * _dot(qb, c_old.astype(BF16)) + _dot(s.astype(BF16), vb)
    den = a * jnp.sum(q * n_old, axis=-1, keepdims=True) + jnp.sum(s, axis=-1, keepdims=True)
    h = num / jnp.maximum(jnp.abs(den), jnp.exp(-m_t))
    h = _rms(h, gain_ref[0]) * jax.nn.sigmoid(o_ref[...])
    h_ref[...] = h.astype(h_ref.dtype)

    b_end = b_col[chunk - 1:chunk, :]
    m_new = m_t[chunk - 1:chunk, :]
    a_end = a[chunk - 1:chunk, :]
    w_col = jnp.exp(b_end - b_col + ig_col - m_new)
    c_ref[0, 0] = a_end * c_old + _dot_tn(kb, (w_col * v).astype(BF16))
    n_ref[0, 0] = a_end * n_old + jnp.sum(w_col * k, axis=0, keepdims=True)
    m_ref[0, 0] = m_new


def mlstm_prompt(z, gates_row, b_i, b_f, gain, lay, *, batch, seq, chunk):
    heads, dqk, dv = lay["ml_heads"], lay["ml_dqk"], lay["ml_dv"]
    nc = seq // chunk
    kq, kk = lay["q"] // dqk, lay["k"] // dqk
    kv, ko = lay["v"] // dv, lay["o"] // dv
    kmisc = lay["misc"] // LANES
    row = lambda b, h, c: b * nc + c
    blocks = [((chunk, dqk), F32)] * 2 + [((chunk, dv), F32)] * 2 + [((chunk, LANES), F32),
              ((chunk, dv), BF16), ((dqk, dv), F32)]
    smem = pl.BlockSpec(memory_space=pltpu.SMEM)
    body = functools.partial(_mlstm_prompt_body, chunk=chunk, heads=heads,
                             gate_lane=lay["misc_gate_lane"], k_scale=dqk ** -0.5)
    return pl.pallas_call(
        body,
        grid=(batch, heads, nc),
        in_specs=[smem, smem,
                  pl.BlockSpec((chunk, dqk), lambda b, h, c: (row(b, h, c), kq + h)),
                  pl.BlockSpec((chunk, dqk), lambda b, h, c: (row(b, h, c), kk + h)),
                  pl.BlockSpec((chunk, dv), lambda b, h, c: (row(b, h, c), kv + h)),
                  pl.BlockSpec((chunk, dv), lambda b, h, c: (row(b, h, c), ko + h)),
                  pl.BlockSpec((chunk, LANES), lambda b, h, c: (row(b, h, c), kmisc)),
                  pl.BlockSpec((1, 1, 2, chunk), lambda b, h, c: (b, h, 0, c)),
                  pl.BlockSpec((1, 1, dv), lambda b, h, c: (h, 0, 0))],
        out_specs=[pl.BlockSpec((chunk, dv), lambda b, h, c: (row(b, h, c), h)),
                   pl.BlockSpec((1, 1, dqk, dv), lambda b, h, c: (b, h, 0, 0)),
                   pl.BlockSpec((1, 1, 1, dqk), lambda b, h, c: (b, h, 0, 0)),
                   pl.BlockSpec((1, 1, 1, 1), lambda b, h, c: (b, h, 0, 0))],
        out_shape=[jax.ShapeDtypeStruct((batch * seq, heads * dv), BF16),
                   jax.ShapeDtypeStruct((batch, heads, dqk, dv), F32),
                   jax.ShapeDtypeStruct((batch, heads, 1, dqk), F32),
                   jax.ShapeDtypeStruct((batch, heads, 1, 1), F32)],
        compiler_params=_params(("parallel", "parallel", "arbitrary"), blocks,
                                8 * _nbytes((chunk, chunk), F32)),
        name="mlstm_prompt",
    )(b_i, b_f, z, z, z, z, z, gates_row, gain.reshape(heads, 1, dv))


def _mlstm_sample_body(bi_ref, bf_ref, qk_ref, v_ref, o_ref, zm_ref, gain_ref, c_ref, n_ref, m_ref,
                       h_ref, co_ref, no_ref, mo_ref, *, heads, dqk, dv, gate_lane, k_scale):
    qk = qk_ref[0]
    q_rows = qk[0:heads, :]
    k_rows = qk[heads:2 * heads, :] * k_scale
    pad = jnp.zeros((dqk - 2 * heads, dqk), F32)
    cols = jnp.concatenate([q_rows, k_rows, pad], axis=0).T
    zm = zm_ref[0]
    ig = _softcap(zm[:, gate_lane:gate_lane + heads] + bi_ref[...])
    fg = _log_sigmoid(_softcap(zm[:, gate_lane + heads:gate_lane + 2 * heads] + bf_ref[...]))
    loga = fg + m_ref[0]
    m_t = jnp.maximum(loga, ig)
    dm = jnp.exp(ig - m_t)
    a = jnp.exp(loga - m_t)
    floor = jnp.exp(-m_t)
    mo_ref[0] = m_t
    v_all = v_ref[0]
    o_all = o_ref[0]
    n_all = n_ref[0]
    for h in range(heads):
        a_h, dm_h = a[:, h:h + 1], dm[:, h:h + 1]
        q_row, k_row = q_rows[h:h + 1, :], k_rows[h:h + 1, :]
        q_col, k_col = cols[:, h:h + 1], cols[:, heads + h:heads + h + 1]
        v_h, n_h = v_all[h:h + 1, :], n_all[h:h + 1, :]
        c_old = c_ref[0, h]
        s = jnp.sum(q_row * k_row, axis=-1, keepdims=True) * dm_h
        num = a_h * jnp.sum(c_old * q_col, axis=0, keepdims=True) + s * v_h
        den = a_h * jnp.sum(q_row * n_h, axis=-1, keepdims=True) + s
        hh = num / jnp.maximum(jnp.abs(den), floor[:, h:h + 1])
        hh = _rms(hh, gain_ref[h:h + 1, :]) * jax.nn.sigmoid(o_all[h:h + 1, :])
        h_ref[0, :, h * dv:(h + 1) * dv] = hh.astype(h_ref.dtype)
        co_ref[0, h] = a_h * c_old + (dm_h * k_col) * v_h
        no_ref[0, h:h + 1, :] = a_h * n_h + dm_h * k_row


def mlstm_sample(qk, v, o, zmisc, b_i, b_f, gain, state_c, state_n, state_m, layer, lay):
    heads, dqk, dv = lay["ml_heads"], lay["ml_dqk"], lay["ml_dv"]
    db = qk.shape[0]
    blocks = [((heads, dqk, dv), F32)] * 2 + [((2 * heads, dqk), F32), ((heads, dv), F32)]
    body = functools.partial(_mlstm_sample_body, heads=heads, dqk=dqk, dv=dv,
                             gate_lane=lay["misc_gate_lane"], k_scale=dqk ** -0.5)
    return pl.pallas_call(
        body,
        grid=(db,),
        in_specs=[pl.BlockSpec((1, heads), lambda b: (0, 0)),
                  pl.BlockSpec((1, heads), lambda b: (0, 0)),
                  pl.BlockSpec((1, 2 * heads, dqk), lambda b: (b, 0, 0)),
                  pl.BlockSpec((1, heads, dv), lambda b: (b, 0, 0)),
                  pl.BlockSpec((1, heads, dv), lambda b: (b, 0, 0)),
                  pl.BlockSpec((1, 1, LANES), lambda b: (b, 0, 0)),
                  pl.BlockSpec((heads, dv), lambda b: (0, 0)),
                  pl.BlockSpec((None, 1, heads, dqk, dv), lambda b: (layer, b, 0, 0, 0)),
                  pl.BlockSpec((None, 1, heads, dqk), lambda b: (layer, b, 0, 0)),
                  pl.BlockSpec((None, 1, 1, heads), lambda b: (layer, b, 0, 0))],
        out_specs=[pl.BlockSpec((1, 1, heads * dv), lambda b: (b, 0, 0)),
                   pl.BlockSpec((1, heads, dqk, dv), lambda b: (b, 0, 0, 0)),
                   pl.BlockSpec((1, heads, dqk), lambda b: (b, 0, 0)),
                   pl.BlockSpec((1, 1, heads), lambda b: (b, 0, 0))],
        out_shape=[jax.ShapeDtypeStruct((db, 1, heads * dv), BF16),
                   jax.ShapeDtypeStruct((db, heads, dqk, dv), F32),
                   jax.ShapeDtypeStruct((db, heads, dqk), F32),
                   jax.ShapeDtypeStruct((db, 1, heads), F32)],
        compiler_params=_params(("parallel",), blocks),
        name="mlstm_sample",
    )(b_i.reshape(1, heads), b_f.reshape(1, heads), qk, v, o, zmisc, gain,
      state_c, state_n, state_m.reshape(state_m.shape[0], db, 1, heads))


def _kv_prep_body(ckv_ref, zm_ref, g_ref, cos_ref, sin_ref, c32_ref, c16_ref, kr32_ref, kr16_ref, *, rope_dim):
    c = _rms(ckv_ref[...], g_ref[...])
    c32_ref[...] = c
    c16_ref[...] = c.astype(BF16)
    x = zm_ref[...]
    kr = (x * cos_ref[...] + _rope_rotate(x) * sin_ref[...])[:, :rope_dim]
    kr32_ref[...] = kr
    kr16_ref[...] = kr.astype(BF16)


def kv_prep(z, g, cos, sin, lay, *, rows, tm, table_blocks):
    kvw, rd = lay["kv_lora"], lay["rope_dim"]
    jc, jm = lay["ckv"] // kvw, lay["misc"] // LANES
    blocks = [((tm, kvw), F32)] * 2 + [((tm, LANES), F32)] * 5
    return pl.pallas_call(
        functools.partial(_kv_prep_body, rope_dim=rd),
        grid=(rows // tm,),
        in_specs=[pl.BlockSpec((tm, kvw), lambda i: (i, jc)),
                  pl.BlockSpec((tm, LANES), lambda i: (i, jm)),
                  pl.BlockSpec((1, kvw), lambda i: (0, 0)),
                  pl.BlockSpec((tm, LANES), lambda i: (i % table_blocks, 0)),
                  pl.BlockSpec((tm, LANES), lambda i: (i % table_blocks, 0))],
        out_specs=[pl.BlockSpec((tm, kvw), lambda i: (i, 0)),
                   pl.BlockSpec((tm, kvw), lambda i: (i, 0)),
                   pl.BlockSpec((tm, rd), lambda i: (i, 0)),
                   pl.BlockSpec((tm, rd), lambda i: (i, 0))],
        out_shape=[jax.ShapeDtypeStruct((rows, kvw), F32), jax.ShapeDtypeStruct((rows, kvw), BF16),
                   jax.ShapeDtypeStruct((rows, rd), F32), jax.ShapeDtypeStruct((rows, rd), BF16)],
        compiler_params=_params(("parallel",), blocks),
        name="mla_kv_prep",
    )(z, z, g.reshape(1, kvw), cos, sin)


def _q_prep_body(cq_ref, g_ref, wuq_ref, wuk_ref, cos_ref, sin_ref, qlat_ref, qrope_ref,
                 *, heads, nope, rope_dim, scale):
    hq = _rms(cq_ref[...], g_ref[...]).astype(BF16)
    q = _dot(hq, wuq_ref[...])
    for h in range(heads):
        qn = q[:, h * nope:(h + 1) * nope].astype(BF16)
        qlat_ref[h] = (_dot(qn, wuk_ref[h]) * scale).astype(qlat_ref.dtype)
    cos = cos_ref[...]
    sin = sin_ref[...]
    base = heads * nope
    per_tile = LANES // rope_dim
    for p in range(heads // per_tile):
        x = q[:, base + p * LANES: base + (p + 1) * LANES]
        r = ((x * cos + _rope_rotate(x) * sin) * scale).astype(qrope_ref.dtype)
        for t in range(per_tile):
            qrope_ref[p * per_tile + t] = r[:, t * rope_dim:(t + 1) * rope_dim]


def q_prep(z, g, wuq, wuk, cos, sin, lay, *, rows, tm, table_blocks):
    heads, nope, rd, kvw, ql = lay["mla_heads"], lay["nope"], lay["rope_dim"], lay["kv_lora"], lay["q_lora"]
    jq = lay["cq"] // ql
    blocks = [((tm, ql), F32), ((ql, heads * (nope + rd)), BF16), ((heads, nope, kvw), BF16),
              ((heads, tm, kvw), BF16), ((heads, tm, LANES), BF16)]
    return pl.pallas_call(
        functools.partial(_q_prep_body, heads=heads, nope=nope, rope_dim=rd, scale=(nope + rd) ** -0.5),
        grid=(rows // tm,),
        in_specs=[pl.BlockSpec((tm, ql), lambda i: (i, jq)),
                  pl.BlockSpec((1, ql), lambda i: (0, 0)),
                  pl.BlockSpec((ql, heads * (nope + rd)), lambda i: (0, 0)),
                  pl.BlockSpec((heads, nope, kvw), lambda i: (0, 0, 0)),
                  pl.BlockSpec((tm, LANES), lambda i: (i % table_blocks, 0)),
                  pl.BlockSpec((tm, LANES), lambda i: (i % table_blocks, 0))],
        out_specs=[pl.BlockSpec((heads, tm, kvw), lambda i: (0, i, 0)),
                   pl.BlockSpec((heads, tm, rd), lambda i: (0, i, 0))],
        out_shape=[jax.ShapeDtypeStruct((heads, rows, kvw), BF16),
                   jax.ShapeDtypeStruct((heads, rows, rd), BF16)],
        compiler_params=_params(("parallel",), blocks, _nbytes((tm, heads * (nope + rd)), F32)),
        name="mla_q_prep",
    )(z, g.reshape(1, ql), wuq, wuk, cos, sin)


def _mla_prompt_body(ql_ref, qr_ref, kc_ref, kr_ref, wuv_ref, o_ref, m_scr, l_scr, acc_scr,
                     *, heads, tq, tk, vdim):
    i = pl.program_id(1)
    kvw = ql_ref.shape[-1]
    q1 = ql_ref[...].reshape(heads * tq, kvw)
    q2 = qr_ref[...].reshape(heads * tq, qr_ref.shape[-1])
    m_scr[...] = jnp.full_like(m_scr, -jnp.inf)
    l_scr[...] = jnp.zeros_like(l_scr)
    acc_scr[...] = jnp.zeros_like(acc_scr)

    def update(start, masked):
        kc = kc_ref[pl.ds(start, tk), :]
        kr = kr_ref[pl.ds(start, tk), :]
        s = _dot_nt(q1, kc) + _dot_nt(q2, kr)
        if masked:
            q_pos = i * tq + lax.broadcasted_iota(jnp.int32, (tq, tk), 0)
            k_pos = start + lax.broadcasted_iota(jnp.int32, (tq, tk), 1)
            s = jnp.where((k_pos <= q_pos)[None], s.reshape(heads, tq, tk), -jnp.inf).reshape(heads * tq, tk)
        m_old = m_scr[...]
        m_new = jnp.maximum(m_old, jnp.max(s, axis=-1, keepdims=True))
        alpha = jnp.exp(m_old - m_new)
        p = jnp.exp(s - m_new)
        l_scr[...] = l_scr[...] * alpha + jnp.sum(p, axis=-1, keepdims=True)
        acc_scr[...] = acc_scr[...] * alpha + _dot(p.astype(BF16), kc)
        m_scr[...] = m_new

    n_full = (i * tq) // tk

    def full_step(j, carry):
        update(pl.multiple_of(j * tk, tk), False)
        return carry

    lax.fori_loop(0, n_full, full_step, 0)
    update(pl.multiple_of(n_full * tk, tk), True)
    o = acc_scr[...] / l_scr[...]
    for h in range(heads):
        oh = o[h * tq:(h + 1) * tq, :].astype(BF16)
        o_ref[:, h * vdim:(h + 1) * vdim] = _dot(oh, wuv_ref[h]).astype(o_ref.dtype)


def mla_prompt(qlat, qrope, kc, kr, wuv, lay, *, batch, seq, tq, tk):
    heads, kvw, rd, vdim = lay["mla_heads"], lay["kv_lora"], lay["rope_dim"], lay["v_dim"]
    nq = seq // tq
    assert tk % tq == 0 and seq % tk == 0
    blocks = [((heads, tq, kvw), BF16), ((heads, tq, LANES), BF16), ((seq, kvw), BF16), ((seq, LANES), BF16),
              ((heads, kvw, vdim), BF16), ((tq, heads * vdim), BF16)]
    scratch = _nbytes((heads * tq, kvw), F32) + 2 * _nbytes((heads * tq, LANES), F32)
    body = functools.partial(_mla_prompt_body, heads=heads, tq=tq, tk=tk, vdim=vdim)
    return pl.pallas_call(
        body,
        grid=(batch, nq),
        in_specs=[pl.BlockSpec((heads, tq, kvw), lambda b, i: (0, b * nq + i, 0)),
                  pl.BlockSpec((heads, tq, rd), lambda b, i: (0, b * nq + i, 0)),
                  pl.BlockSpec((seq, kvw), lambda b, i: (b, 0)),
                  pl.BlockSpec((seq, rd), lambda b, i: (b, 0)),
                  pl.BlockSpec((heads, kvw, vdim), lambda b, i: (0, 0, 0))],
        out_specs=pl.BlockSpec((tq, heads * vdim), lambda b, i: (b * nq + i, 0)),
        out_shape=jax.ShapeDtypeStruct((batch * seq, heads * vdim), BF16),
        scratch_shapes=[pltpu.VMEM((heads * tq, 1), F32), pltpu.VMEM((heads * tq, 1), F32),
                        pltpu.VMEM((heads * tq, kvw), F32)],
        compiler_params=_params(("parallel", "arbitrary"), blocks,
                                scratch + 4 * _nbytes((heads * tq, tk), F32)),
        name="mla_prompt",
    )(qlat, qrope, kc, kr, wuv)


def _mla_sample_body(pt_ref, ql_ref, qr_ref, cn_ref, krn_ref, lat_hbm, kr_hbm, o_ref,
                     lat_buf, kr_buf, sem, m_scr, l_scr, acc_scr, *, layer, pages):
    page = lat_buf.shape[2]
    c = pl.program_id(1)
    n_chunks = pl.num_programs(1)
    step = pl.program_id(0) * n_chunks + c
    total = pl.num_programs(0) * n_chunks
    slot = step % 2

    def chunk_copies(chunk_step, sl):
        copies = []
        for j in range(pages):
            pg = pt_ref[chunk_step * pages + j]
            copies.append(pltpu.make_async_copy(lat_hbm.at[layer, pg], lat_buf.at[sl, j], sem.at[0, sl]))
            copies.append(pltpu.make_async_copy(kr_hbm.at[layer, pg], kr_buf.at[sl, :, pl.ds(j * page, page)],
                                                sem.at[1, sl]))
        return copies

    @pl.when(step == 0)
    def _():
        for cp in chunk_copies(0, 0):
            cp.start()

    @pl.when(step + 1 < total)
    def _():
        for cp in chunk_copies(step + 1, 1 - slot):
            cp.start()

    ql = ql_ref[0]
    qr = qr_ref[0]

    @pl.when(c == 0)
    def _():
        cn = cn_ref[0].astype(F32)
        krn = krn_ref[0].astype(F32)
        m_scr[...] = (jnp.sum(ql.astype(F32) * cn, axis=-1, keepdims=True)
                      + jnp.sum(qr.astype(F32) * krn, axis=-1, keepdims=True))
        l_scr[...] = jnp.ones_like(l_scr)
        acc_scr[...] = jnp.broadcast_to(cn, acc_scr.shape)

    for cp in chunk_copies(step, slot):
        cp.wait()

    keys = lat_buf[slot].reshape(pages * page, lat_buf.shape[-1]).astype(BF16)
    s = _dot_nt(ql, keys) + _dot(qr, kr_buf[slot].astype(BF16))
    m_old = m_scr[...]
    m_new = jnp.maximum(m_old, jnp.max(s, axis=-1, keepdims=True))
    alpha = jnp.exp(m_old - m_new)
    p = jnp.exp(s - m_new)
    l_new = l_scr[...] * alpha + jnp.sum(p, axis=-1, keepdims=True)
    acc = acc_scr[...] * alpha + _dot(p.astype(BF16), keys)
    m_scr[...] = m_new
    l_scr[...] = l_new
    acc_scr[...] = acc

    @pl.when(c == n_chunks - 1)
    def _():
        o_ref[0] = (acc / l_new).astype(o_ref.dtype)


def mla_sample(qlat, qrope, c_new, kr_new, cache_lat, cache_kr_t, page_table, layer, lay, *, pages):
    db, heads, kvw = qlat.shape
    rd = qrope.shape[-1]
    n_pages = page_table.shape[1]
    page = cache_lat.shape[2]
    assert n_pages % pages == 0
    buffers = _nbytes((2, pages, page, kvw), F32) + _nbytes((2, pages, rd, page), F32)
    blocks = [((heads, kvw), BF16), ((heads, LANES), BF16), ((heads, kvw), F32)]
    body = functools.partial(_mla_sample_body, layer=layer, pages=pages)
    hbm = pl.BlockSpec(memory_space=pl.ANY)
    grid_spec = pltpu.PrefetchScalarGridSpec(
        num_scalar_prefetch=1,
        grid=(db, n_pages // pages),
        in_specs=[pl.BlockSpec((1, heads, kvw), lambda b, c, pt: (b, 0, 0)),
                  pl.BlockSpec((1, heads, rd), lambda b, c, pt: (b, 0, 0)),
                  pl.BlockSpec((1, 1, kvw), lambda b, c, pt: (b, 0, 0)),
                  pl.BlockSpec((1, 1, rd), lambda b, c, pt: (b, 0, 0)),
                  hbm, hbm],
        out_specs=pl.BlockSpec((1, heads, kvw), lambda b, c, pt: (b, 0, 0)),
        scratch_shapes=[pltpu.VMEM((2, pages, page, kvw), F32), pltpu.VMEM((2, rd, pages * page), F32),
                        pltpu.SemaphoreType.DMA((2, 2)),
                        pltpu.VMEM((heads, 1), F32), pltpu.VMEM((heads, 1), F32),
                        pltpu.VMEM((heads, kvw), F32)])
    return pl.pallas_call(
        body,
        grid_spec=grid_spec,
        out_shape=jax.ShapeDtypeStruct((db, heads, kvw), BF16),
        compiler_params=_params(("arbitrary", "arbitrary"), blocks, buffers),
        name="mla_sample",
    )(page_table.reshape(-1), qlat, qrope, c_new, kr_new, cache_lat, cache_kr_t)


def _uv_body(o_ref, w_ref, h_ref):
    h_ref[...] = _dot(o_ref[0], w_ref[0]).astype(h_ref.dtype)


def mla_up_v(o_lat, wuv):
    heads, m, kvw = o_lat.shape
    vdim = wuv.shape[-1]
    blocks = [((m, kvw), BF16), ((kvw, vdim), BF16), ((m, vdim), BF16)]
    return pl.pallas_call(
        _uv_body,
        grid=(heads,),
        in_specs=[pl.BlockSpec((1, m, kvw), lambda h: (h, 0, 0)),
                  pl.BlockSpec((1, kvw, vdim), lambda h: (h, 0, 0))],
        out_specs=pl.BlockSpec((m, vdim), lambda h: (0, h)),
        out_shape=jax.ShapeDtypeStruct((m, heads * vdim), BF16),
        compiler_params=_params(("parallel",), blocks),
        name="mla_up_v",
    )(o_lat, wuv)


def _mem_attend_prompt_body(q_ref, k_ref, v_ref, o_ref, *, heads, dim):
    q = q_ref[...]
    k = k_ref[...].astype(BF16)
    v = v_ref[...].astype(BF16)
    scale = dim ** -0.5
    for h in range(heads):
        sl = slice(h * dim, (h + 1) * dim)
        s = _dot_nt(q[:, sl], k[:, sl]) * scale
        p = jnp.exp(s - jnp.max(s, axis=-1, keepdims=True))
        p = p / jnp.sum(p, axis=-1, keepdims=True)
        o_ref[:, sl] = _dot(p.astype(BF16), v[:, sl]).astype(o_ref.dtype)


def mem_attend_prompt(q, kv, lay, *, batch, seq, tq):
    heads, dim, n_mem = lay["mem_heads"], lay["mem_dim"], lay["n_mem"]
    w = heads * dim
    nq = seq // tq
    blocks = [((tq, w), BF16), ((n_mem, w), F32), ((n_mem, w), F32), ((tq, w), BF16)]
    body = functools.partial(_mem_attend_prompt_body, heads=heads, dim=dim)
    return pl.pallas_call(
        body,
        grid=(batch, nq),
        in_specs=[pl.BlockSpec((tq, w), lambda b, i: (b * nq + i, 0)),
                  pl.BlockSpec((n_mem, w), lambda b, i: (b, 0)),
                  pl.BlockSpec((n_mem, w), lambda b, i: (b, 1))],
        out_specs=pl.BlockSpec((tq, w), lambda b, i: (b * nq + i, 0)),
        out_shape=jax.ShapeDtypeStruct((batch * seq, w), BF16),
        compiler_params=_params(("parallel", "parallel"), blocks, 4 * _nbytes((tq, n_mem), F32)),
        name="mem_attend_prompt",
    )(q, kv, kv)


def _mem_attend_sample_body(q_ref, k_ref, v_ref, o_ref, *, group, heads, dim):
    rows = q_ref.shape[1]
    cols = k_ref.shape[1]
    col_head = lax.broadcasted_iota(jnp.int32, (rows, cols), 1) % heads
    row_head = lax.broadcasted_iota(jnp.int32, (rows, cols), 0) % heads
    own_head = col_head == row_head
    scale = dim ** -0.5
    for g in range(group):
        k = k_ref[g].astype(BF16)
        v = v_ref[g].astype(BF16)
        s = jnp.where(own_head, _dot_nt(q_ref[g], k) * scale, -jnp.inf)
        p = jnp.exp(s - jnp.max(s, axis=-1, keepdims=True))
        p = p / jnp.sum(p, axis=-1, keepdims=True)
        o_ref[g] = _dot(p.astype(BF16), v).astype(o_ref.dtype)


def mem_attend_sample(q, cache_k, cache_v, layer, lay, *, group):
    heads, dim = lay["mem_heads"], lay["mem_dim"]
    db, rows, _ = q.shape
    cols = cache_k.shape[2]
    blocks = [((group, cols, dim), F32)] * 2 + [((group, rows, dim), BF16)] * 2
    body = functools.partial(_mem_attend_sample_body, group=group, heads=heads, dim=dim)
    return pl.pallas_call(
        body,
        grid=(db // group,),
        in_specs=[pl.BlockSpec((group, rows, dim), lambda b: (b, 0, 0)),
                  pl.BlockSpec((None, group, cols, dim), lambda b: (layer, b, 0, 0)),
                  pl.BlockSpec((None, group, cols, dim), lambda b: (layer, b, 0, 0))],
        out_specs=pl.BlockSpec((group, rows, dim), lambda b: (b, 0, 0)),
        out_shape=jax.ShapeDtypeStruct((db, rows, dim), BF16),
        compiler_params=_params(("parallel",), blocks),
        name="mem_attend_sample",
    )(q, cache_k, cache_v)


def _layout(shapes):
    lay = dict(shapes)
    qk_w = lay["ml_heads"] * lay["ml_dqk"]
    v_w = lay["ml_heads"] * lay["ml_dv"]
    d = lay["d_model"]
    off = 0
    for name, width in (("q", qk_w), ("k", qk_w), ("v", v_w), ("o", v_w), ("cq", lay["q_lora"]),
                        ("ga", d), ("gb", d), ("ckv", lay["kv_lora"]), ("misc", LANES)):
        lay[name] = off
        off += width
    lay["z_cols"] = off
    lay["misc_gate_lane"] = lay["rope_dim"]
    assert lay["rope_dim"] + 2 * lay["ml_heads"] <= LANES
    return lay


def _repack_w_in(w, lay):
    heads = lay["ml_heads"]
    qk_w = heads * lay["ml_dqk"]
    v_w = heads * lay["ml_dv"]
    d = lay["d_model"]
    o = 0
    src = {}
    for name, width in (("q", qk_w), ("k", qk_w), ("v", v_w), ("i", heads), ("f", heads), ("o", v_w),
                        ("cq", lay["q_lora"]), ("ckv", lay["kv_lora"]), ("kr", lay["rope_dim"]),
                        ("ga", d), ("gb", d)):
        src[name] = (o, o + width)
        o += width
    pad = LANES - lay["rope_dim"] - 2 * heads
    parts = [w[:, src[n][0]:src[n][1]] for n in ("q", "k", "v", "o", "cq", "ga", "gb", "ckv", "kr", "i", "f")]
    parts.append(jnp.zeros((w.shape[0], pad), w.dtype))
    return jnp.concatenate(parts, axis=1).astype(BF16)


def _repack_w_uq(w, lay):
    heads, nope, rd = lay["mla_heads"], lay["nope"], lay["rope_dim"]
    w3 = w.reshape(w.shape[0], heads, nope + rd)
    return jnp.concatenate([w3[:, :, :nope].reshape(w.shape[0], heads * nope),
                            w3[:, :, nope:].reshape(w.shape[0], heads * rd)], axis=1).astype(BF16)


def _rope_tables(pos, rope_dim):
    half = rope_dim // 2
    inv = ROPE_THETA ** (-jnp.arange(half, dtype=F32) / half)
    ang = pos.astype(F32)[:, None] * inv[None, :]
    reps = LANES // half
    return jnp.tile(jnp.cos(ang), (1, reps)), jnp.tile(jnp.sin(ang), (1, reps))


def _tile(n, want):
    if n <= want:
        return n
    t = want - want % BF16_ROWS
    while n % t:
        t -= BF16_ROWS
    return t


def kernel(x_prompt, mem_prompt, x_sample, state_mlstm_C, state_mlstm_n, state_mlstm_m, cache_mla_latent, cache_mla_krope, cache_mem_k, cache_mem_v, page_table, norm_ffn1, ffn1_w_gate, ffn1_w_up, ffn1_w_down, norm_mix, w_in, mlstm_b_i, mlstm_b_f, mlstm_norm, mla_norm_q, mla_w_uq, mla_norm_kv, mla_w_uk, mla_w_uv, w_branch_mlstm, w_branch_mla, w_out, norm_mem, norm_mem_src, mem_w_q, mem_w_k, mem_w_v, mem_w_o, norm_ffn2, ffn2_w_gate, ffn2_w_up, ffn2_w_down, norm_final):
    batch, seq, d_model = x_prompt.shape
    db, dec_seq, _ = x_sample.shape
    assert dec_seq == 1
    depth = norm_ffn1.shape[0]
    n_mem = mem_prompt.shape[1]
    lay = _layout(dict(
        d_model=d_model, ml_heads=state_mlstm_C.shape[2], ml_dqk=state_mlstm_C.shape[3],
        ml_dv=state_mlstm_C.shape[4], mla_heads=mla_w_uk.shape[2], nope=mla_w_uk.shape[3],
        q_lora=mla_w_uq.shape[1], kv_lora=mla_w_uk.shape[1], rope_dim=cache_mla_krope.shape[3],
        v_dim=mla_w_uv.shape[3], mem_heads=cache_mem_k.shape[3], mem_dim=cache_mem_k.shape[4], n_mem=n_mem))
    heads, dqk, dv = lay["ml_heads"], lay["ml_dqk"], lay["ml_dv"]
    mem_heads, mem_dim = lay["mem_heads"], lay["mem_dim"]
    mem_w = mem_heads * mem_dim
    rd, kvw = lay["rope_dim"], lay["kv_lora"]
    mp = batch * seq
    m_all = mp + db

    tm = _tile(m_all, 1024)
    tr = _tile(m_all, 320)
    tn = 512
    tn_in = 640 if lay["z_cols"] % 640 == 0 else LANES
    chunk = min(256, seq)
    tm_prep = _tile(seq, 256)
    tk_mla = min(512, seq)
    tq_mla = min(128, seq)

    past_len = page_table.shape[1] * cache_mla_latent.shape[2]
    cos_p, sin_p = _rope_tables(jnp.arange(seq), rd)
    cos_s, sin_s = _rope_tables(jnp.full((db,), past_len), rd)

    cache_k4 = cache_mem_k.reshape(depth, db, n_mem * mem_heads, mem_dim)
    cache_v4 = cache_mem_v.reshape(depth, db, n_mem * mem_heads, mem_dim)
    cache_kr_t = jnp.swapaxes(cache_mla_krope, 2, 3)
    mem_src = mem_prompt.reshape(batch * n_mem, d_model)

    x = jnp.concatenate([x_prompt.reshape(mp, d_model), x_sample.reshape(db, d_model)], axis=0)
    out_p = [[] for _ in range(7)]
    out_s = [[] for _ in range(5)]
    for l in range(depth):
        w_in_l = _repack_w_in(w_in[l], lay)
        wuq = _repack_w_uq(mla_w_uq[l], lay)
        wuk = jnp.transpose(mla_w_uk[l], (1, 2, 0)).astype(BF16)
        wuv = jnp.transpose(mla_w_uv[l], (1, 0, 2)).astype(BF16)
        w_mem_kv = jnp.concatenate([mem_w_k[l], mem_w_v[l]], axis=1)

        h = rmsnorm(x, norm_ffn1[l], BF16, tm=tr)
        u = swiglu_up(h, ffn1_w_gate[l].astype(BF16), ffn1_w_up[l].astype(BF16), tm=tm, tn=tn)
        x = matmul_residual(u, ffn1_w_down[l].astype(BF16), x, 0.5, tm=tm, tn=tn)
        h = rmsnorm(x, norm_mix[l], BF16, tm=tr)
        z = matmul(h, w_in_l, F32, tm=tm, tn=tn_in)
        zs = z[mp:]

        hm_src = rmsnorm(mem_src, norm_mem_src[l], BF16, tm=_tile(batch * n_mem, 256))
        mem_kv = matmul(hm_src, w_mem_kv, F32, tm=_tile(batch * n_mem, 1024), tn=tn)
        gate_cols = z[:mp, lay["misc"] + rd: lay["misc"] + rd + 2 * heads]
        gates_row = gate_cols.reshape(batch, seq, 2, heads).transpose(0, 3, 2, 1)
        hm_p, p_c, p_n, p_m = mlstm_prompt(z, gates_row, mlstm_b_i[l], mlstm_b_f[l], mlstm_norm[l], lay,
                                           batch=batch, seq=seq, chunk=chunk)
        c32, c16, kr32, kr16 = kv_prep(z, mla_norm_kv[l], cos_p, sin_p, lay, rows=mp, tm=tm_prep,
                                       table_blocks=seq // tm_prep)
        qlat, qrope = q_prep(z, mla_norm_q[l], wuq, wuk, cos_p, sin_p, lay, rows=mp, tm=tm_prep,
                             table_blocks=seq // tm_prep)
        ha_p = mla_prompt(qlat, qrope, c16, kr16, wuv, lay, batch=batch, seq=seq, tq=tq_mla, tk=tk_mla)
        for lst, a in zip(out_p, (p_c, p_n.reshape(batch, heads, dqk), p_m.reshape(batch, heads),
                                  c32.reshape(batch, seq, kvw), kr32.reshape(batch, seq, rd),
                                  mem_kv[:, :mem_w].reshape(batch, n_mem, mem_heads, mem_dim),
                                  mem_kv[:, mem_w:].reshape(batch, n_mem, mem_heads, mem_dim))):
            lst.append(a)

        qk_s = zs[:, lay["q"]:lay["v"]].reshape(db, 2 * heads, dqk)
        v_s = zs[:, lay["v"]:lay["o"]].reshape(db, heads, dv)
        o_s = zs[:, lay["o"]:lay["cq"]].reshape(db, heads, dv)
        misc_s = zs[:, lay["misc"]:].reshape(db, 1, LANES)
        hm_s, s_c, s_n, s_m = mlstm_sample(qk_s, v_s, o_s, misc_s, mlstm_b_i[l], mlstm_b_f[l], mlstm_norm[l],
                                           state_mlstm_C, state_mlstm_n, state_mlstm_m, l, lay)
        c32, c16, kr32, kr16 = kv_prep(zs, mla_norm_kv[l], cos_s, sin_s, lay, rows=db, tm=db, table_blocks=1)
        qlat, qrope = q_prep(zs, mla_norm_q[l], wuq, wuk, cos_s, sin_s, lay, rows=db, tm=db, table_blocks=1)
        o_lat = mla_sample(qlat.transpose(1, 0, 2), qrope.transpose(1, 0, 2), c16.reshape(db, 1, kvw),
                           kr16.reshape(db, 1, rd), cache_mla_latent, cache_kr_t, page_table, l, lay,
                           pages=min(32, page_table.shape[1]))
        ha_s = mla_up_v(o_lat.transpose(1, 0, 2), wuv)
        for lst, a in zip(out_s, (s_c, s_n, s_m.reshape(db, heads), c32.reshape(db, 1, kvw), kr32.reshape(db, 1, rd))):
            lst.append(a)

        h_m = jnp.concatenate([hm_p, hm_s.reshape(db, heads * dv)], axis=0)
        h_a = jnp.concatenate([ha_p, ha_s], axis=0)
        merged = gated_merge(h_m, h_a, w_branch_mlstm, w_branch_mla, z, lay["ga"], lay["gb"], tm=tm, tn=tn, layer=l)
        x = matmul_residual(merged, w_out, x, 1.0, tm=tm, tn=tn, layer=l)
        h = rmsnorm(x, norm_mem[l], BF16, tm=tr)
        q = matmul(h, mem_w_q, BF16, tm=tm, tn=tn, layer=l)
        o_p = mem_attend_prompt(q, mem_kv, lay, batch=batch, seq=seq, tq=_tile(seq, 512))
        q_s = jnp.pad(q[mp:].reshape(db, mem_heads, mem_dim), ((0, 0), (0, BF16_ROWS - mem_heads), (0, 0)))
        o_s = mem_attend_sample(q_s, cache_k4, cache_v4, l, lay, group=math.gcd(db, 4))
        o = jnp.concatenate([o_p, o_s[:, :mem_heads].reshape(db, mem_w)], axis=0)
        x = matmul_residual(o, mem_w_o, x, 1.0, tm=tm, tn=tn, layer=l)
        h = rmsnorm(x, norm_ffn2[l], BF16, tm=tr)
        u = swiglu_up(h, ffn2_w_gate[l].astype(BF16), ffn2_w_up[l].astype(BF16), tm=tm, tn=tn)
        x = matmul_residual(u, ffn2_w_down[l].astype(BF16), x, 0.5, tm=tm, tn=tn)

    y_prompt = rmsnorm(x, norm_final, F32, tm=_tile(mp, 256), rows=mp).reshape(batch, seq, d_model)
    y_sample = rmsnorm(x, norm_final, F32, tm=math.gcd(mp, db), row0=mp, rows=db).reshape(db, 1, d_model)
    return (y_prompt, y_sample) + tuple(jnp.stack(a) for a in out_p) + tuple(jnp.stack(a) for a in out_s)
```

```python
import functools
import math

import jax
import jax.numpy as jnp
from jax import lax
from jax.experimental import pallas as pl
from jax.experimental.pallas import tpu as pltpu

F32 = jnp.float32
BF16 = jnp.bfloat16

EPS = 1e-6
GATE_CAP = 15.0
ROPE_THETA = 10000.0

LANES = 128
BF16_ROWS = 16
VMEM_BYTES = 64 * 1024 * 1024
VMEM_TEMP_BYTES = 12 * 1024 * 1024


def _nbytes(shape, dtype):
    return math.prod(shape) * jnp.dtype(dtype).itemsize


def _params(semantics, blocks, scratch_bytes=0):
    need = 2 * sum(_nbytes(s, d) for s, d in blocks) + scratch_bytes + VMEM_TEMP_BYTES
    return pltpu.CompilerParams(dimension_semantics=semantics,
                                vmem_limit_bytes=min(need, VMEM_BYTES - 4 * 1024 * 1024))


def _dot(a, b):
    return jnp.dot(a, b, preferred_element_type=F32)


def _dot_nt(a, b):
    return lax.dot_general(a, b, (((1,), (1,)), ((), ())), preferred_element_type=F32)


def _dot_tn(a, b):
    return lax.dot_general(a, b, (((0,), (0,)), ((), ())), preferred_element_type=F32)


def _softcap(a):
    return GATE_CAP * jnp.tanh(a / GATE_CAP)


def _log_sigmoid(a):
    return jnp.minimum(a, 0.0) - jnp.log1p(jnp.exp(-jnp.abs(a)))


def _rms(x, g):
    return x * lax.rsqrt(jnp.mean(x * x, axis=-1, keepdims=True) + EPS) * g


def _rope_rotate(x):
    lane = lax.broadcasted_iota(jnp.int32, x.shape, x.ndim - 1)
    first_half = (lane % 64) < 32
    return jnp.where(first_half, -pltpu.roll(x, 96, x.ndim - 1), pltpu.roll(x, 32, x.ndim - 1))


def _rmsnorm_body(x_ref, g_ref, o_ref):
    o_ref[...] = _rms(x_ref[...].astype(F32), g_ref[...].astype(F32)).astype(o_ref.dtype)


def rmsnorm(x, g, out_dtype, *, tm, row0=0, rows=None):
    w = x.shape[1]
    rows = x.shape[0] if rows is None else rows
    i0 = row0 // tm
    assert i0 * tm == row0 and rows % tm == 0
    blocks = [((tm, w), x.dtype), ((tm, w), out_dtype)]
    return pl.pallas_call(
        _rmsnorm_body,
        grid=(rows // tm,),
        in_specs=[pl.BlockSpec((tm, w), lambda i: (i0 + i, 0)),
                  pl.BlockSpec((1, w), lambda i: (0, 0))],
        out_specs=pl.BlockSpec((tm, w), lambda i: (i, 0)),
        out_shape=jax.ShapeDtypeStruct((rows, w), out_dtype),
        compiler_params=_params(("parallel",), blocks),
        name="rmsnorm",
    )(x, g.reshape(1, w))


def _weight_spec(w, layer, tn):
    k = w.shape[-2]
    if w.ndim == 3:
        return pl.BlockSpec((None, k, tn), lambda j, i: (layer, 0, j))
    return pl.BlockSpec((k, tn), lambda j, i: (0, j))


def _weight_scratch(w, tn):
    return [] if w.dtype == BF16 else [pltpu.VMEM((w.shape[-2], tn), BF16)]


def _weight_bytes(w, tn):
    k = w.shape[-2]
    return 2 * _nbytes((k, tn), w.dtype) + (0 if w.dtype == BF16 else _nbytes((k, tn), BF16))


def _resident_bf16(w_ref, w16_ref):
    if w16_ref is None:
        return w_ref[...]

    @pl.when(pl.program_id(1) == 0)
    def _():
        w16_ref[...] = w_ref[...].astype(BF16)

    return w16_ref[...]


def _mm_body(a_ref, w_ref, o_ref, *scratch):
    w = _resident_bf16(w_ref, scratch[0] if scratch else None)
    o_ref[...] = _dot(a_ref[...], w).astype(o_ref.dtype)


def matmul(a, w, out_dtype, *, tm, tn, layer=0):
    m, k = a.shape
    n = w.shape[-1]
    vmem = 2 * _nbytes((tm, k), a.dtype) + _weight_bytes(w, tn) + 3 * _nbytes((tm, tn), F32)
    return pl.pallas_call(
        _mm_body,
        grid=(pl.cdiv(n, tn), m // tm),
        in_specs=[pl.BlockSpec((tm, k), lambda j, i: (i, 0)), _weight_spec(w, layer, tn)],
        out_specs=pl.BlockSpec((tm, tn), lambda j, i: (i, j)),
        out_shape=jax.ShapeDtypeStruct((m, n), out_dtype),
        scratch_shapes=_weight_scratch(w, tn),
        compiler_params=_params(("parallel", "arbitrary"), [], vmem),
        name="matmul",
    )(a, w)


def _mm_residual_body(a_ref, w_ref, r_ref, o_ref, *scratch, scale):
    w = _resident_bf16(w_ref, scratch[0] if scratch else None)
    o_ref[...] = r_ref[...] + scale * _dot(a_ref[...], w)


def matmul_residual(a, w, res, scale, *, tm, tn, layer=0):
    m, k = a.shape
    n = w.shape[-1]
    vmem = 2 * _nbytes((tm, k), a.dtype) + _weight_bytes(w, tn) + 5 * _nbytes((tm, tn), F32)
    return pl.pallas_call(
        functools.partial(_mm_residual_body, scale=scale),
        grid=(n // tn, m // tm),
        in_specs=[pl.BlockSpec((tm, k), lambda j, i: (i, 0)), _weight_spec(w, layer, tn),
                  pl.BlockSpec((tm, tn), lambda j, i: (i, j))],
        out_specs=pl.BlockSpec((tm, tn), lambda j, i: (i, j)),
        out_shape=jax.ShapeDtypeStruct((m, n), F32),
        scratch_shapes=_weight_scratch(w, tn),
        compiler_params=_params(("parallel", "arbitrary"), [], vmem),
        name="matmul_residual",
    )(a, w, res)


def _swiglu_body(a_ref, wg_ref, wu_ref, o_ref, wg16_ref=None, wu16_ref=None):
    a = a_ref[...]
    g = _dot(a, _resident_bf16(wg_ref, wg16_ref))
    u = _dot(a, _resident_bf16(wu_ref, wu16_ref))
    o_ref[...] = (g * jax.nn.sigmoid(g) * u).astype(o_ref.dtype)


def swiglu_up(a, wg, wu, *, tm, tn, layer=0):
    m, k = a.shape
    n = wg.shape[-1]
    vmem = 2 * _nbytes((tm, k), a.dtype) + 2 * _weight_bytes(wg, tn) + 4 * _nbytes((tm, tn), F32)
    return pl.pallas_call(
        _swiglu_body,
        grid=(pl.cdiv(n, tn), m // tm),
        in_specs=[pl.BlockSpec((tm, k), lambda j, i: (i, 0)),
                  _weight_spec(wg, layer, tn), _weight_spec(wu, layer, tn)],
        out_specs=pl.BlockSpec((tm, tn), lambda j, i: (i, j)),
        out_shape=jax.ShapeDtypeStruct((m, n), BF16),
        scratch_shapes=_weight_scratch(wg, tn) + _weight_scratch(wu, tn),
        compiler_params=_params(("parallel", "arbitrary"), [], vmem),
        name="swiglu_up",
    )(a, wg, wu)


def _merge_body(hm_ref, ha_ref, wa_ref, wb_ref, ga_ref, gb_ref, o_ref, wa16_ref=None, wb16_ref=None):
    ya = _dot(hm_ref[...], _resident_bf16(wa_ref, wa16_ref))
    yb = _dot(ha_ref[...], _resident_bf16(wb_ref, wb16_ref))
    o_ref[...] = (jax.nn.sigmoid(ga_ref[...]) * ya + jax.nn.sigmoid(gb_ref[...]) * yb).astype(o_ref.dtype)


def gated_merge(hm, ha, wa, wb, z, ga_col, gb_col, *, tm, tn, layer):
    m, ka = hm.shape
    kb = ha.shape[1]
    n = wa.shape[-1]
    ja, jb = ga_col // tn, gb_col // tn
    assert ja * tn == ga_col and jb * tn == gb_col
    vmem = (2 * _nbytes((tm, ka + kb), BF16) + _weight_bytes(wa, tn) + _weight_bytes(wb, tn)
            + 8 * _nbytes((tm, tn), F32))
    return pl.pallas_call(
        _merge_body,
        grid=(n // tn, m // tm),
        in_specs=[pl.BlockSpec((tm, ka), lambda j, i: (i, 0)),
                  pl.BlockSpec((tm, kb), lambda j, i: (i, 0)),
                  _weight_spec(wa, layer, tn), _weight_spec(wb, layer, tn),
                  pl.BlockSpec((tm, tn), lambda j, i: (i, ja + j)),
                  pl.BlockSpec((tm, tn), lambda j, i: (i, jb + j))],
        out_specs=pl.BlockSpec((tm, tn), lambda j, i: (i, j)),
        out_shape=jax.ShapeDtypeStruct((m, n), BF16),
        scratch_shapes=_weight_scratch(wa, tn) + _weight_scratch(wb, tn),
        compiler_params=_params(("parallel", "arbitrary"), [], vmem),
        name="gated_merge",
    )(hm, ha, wa, wb, z, z)


def _mlstm_prompt_body(bi_ref, bf_ref, q_ref, k_ref, v_ref, o_ref, zm_ref, grow_ref, gain_ref,
                       h_ref, c_ref, n_ref, m_ref, *, chunk, heads, gate_lane, k_scale):
    hd = pl.program_id(1)

    @pl.when(pl.program_id(2) == 0)
    def _():
        c_ref[...] = jnp.zeros_like(c_ref)
        n_ref[...] = jnp.zeros_like(n_ref)
        m_ref[...] = jnp.zeros_like(m_ref)

    b_i = bi_ref[hd]
    b_f = bf_ref[hd]
    zm = zm_ref[...]
    lane = lax.broadcasted_iota(jnp.int32, zm.shape, 1)
    i_col = jnp.sum(jnp.where(lane == gate_lane + hd, zm, 0.0), axis=-1, keepdims=True)
    f_col = jnp.sum(jnp.where(lane == gate_lane + heads + hd, zm, 0.0), axis=-1, keepdims=True)
    grow = grow_ref[0, 0]
    ig_col = _softcap(i_col + b_i)
    fg_col = _log_sigmoid(_softcap(f_col + b_f))
    ig_row = _softcap(grow[0:1, :] + b_i)
    fg_row = _log_sigmoid(_softcap(grow[1:2, :] + b_f))

    t_idx = lax.broadcasted_iota(jnp.int32, (chunk, chunk), 0)
    s_idx = lax.broadcasted_iota(jnp.int32, (chunk, chunk), 1)
    causal = s_idx <= t_idx
    b_col = jnp.sum(jnp.where(causal, fg_row, 0.0), axis=-1, keepdims=True)
    b_row = jnp.sum(jnp.where(t_idx <= s_idx, fg_col, 0.0), axis=0, keepdims=True)

    m_prev = m_ref[0, 0]
    logd = jnp.where(causal, b_col - b_row + ig_row, -jnp.inf)
    loga = b_col + m_prev
    m_t = jnp.maximum(loga, jnp.max(logd, axis=-1, keepdims=True))
    dmat = jnp.exp(logd - m_t)
    a = jnp.exp(loga - m_t)

    q = q_ref[...]
    k = k_ref[...] * k_scale
    v = v_ref[...]
    qb, kb, vb = q.astype(BF16), k.astype(BF16), v.astype(BF16)
    c_old = c_ref[0, 0]
    n_old = n_ref[0, 0]
    s = _dot_nt(qb, kb) * dmat
    num = a * _dot(qb, c_old.astype(BF16)) + _dot(s.astype(BF16), vb)
    den = a * jnp.sum(q * n_old, axis=-1, keepdims=True) + jnp.sum(s, axis=-1, keepdims=True)
    h = num / jnp.maximum(jnp.abs(den), jnp.exp(-m_t))
    h = _rms(h, gain_ref[0]) * jax.nn.sigmoid(o_ref[...])
    h_ref[...] = h.astype(h_ref.dtype)

    b_end = b_col[chunk - 1:chunk, :]
    m_new = m_t[chunk - 1:chunk, :]
    a_end = a[chunk - 1:chunk, :]
    w_col = jnp.exp(b_end - b_col + ig_col - m_new)
    c_ref[0, 0] = a_end * c_old + _dot_tn(kb, (w_col * v).astype(BF16))
    n_ref[0, 0] = a_end * n_old + jnp.sum(w_col * k, axis=0, keepdims=True)
    m_ref[0, 0] = m_new


def mlstm_prompt(z, gates_row, b_i, b_f, gain, lay, *, batch, seq, chunk):
    heads, dqk, dv = lay["ml_heads"], lay["ml_dqk"], lay["ml_dv"]
    nc = seq // chunk
    kq, kk = lay["q"] // dqk, lay["k"] // dqk
    kv, ko = lay["v"] // dv, lay["o"] // dv
    kmisc = lay["misc"] // LANES
    row = lambda b, h, c: b * nc + c
    blocks = [((chunk, dqk), F32)] * 2 + [((chunk, dv), F32)] * 2 + [((chunk, LANES), F32),
              ((chunk, dv), BF16), ((dqk, dv), F32)]
    smem = pl.BlockSpec(memory_space=pltpu.SMEM)
    body = functools.partial(_mlstm_prompt_body, chunk=chunk, heads=heads,
                             gate_lane=lay["misc_gate_lane"], k_scale=dqk ** -0.5)
    return pl.pallas_call(
        body,
        grid=(batch, heads, nc),
        in_specs=[smem, smem,
                  pl.BlockSpec((chunk, dqk), lambda b, h, c: (row(b, h, c), kq + h)),
                  pl.BlockSpec((chunk, dqk), lambda b, h, c: (row(b, h, c), kk + h)),
                  pl.BlockSpec((chunk, dv), lambda b, h, c: (row(b, h, c), kv + h)),
                  pl.BlockSpec((chunk, dv), lambda b, h, c: (row(b, h, c), ko + h)),
                  pl.BlockSpec((chunk, LANES), lambda b, h, c: (row(b, h, c), kmisc)),
                  pl.BlockSpec((1, 1, 2, chunk), lambda b, h, c: (b, h, 0, c)),
                  pl.BlockSpec((1, 1, dv), lambda b, h, c: (h, 0, 0))],
        out_specs=[pl.BlockSpec((chunk, dv), lambda b, h, c: (row(b, h, c), h)),
                   pl.BlockSpec((1, 1, dqk, dv), lambda b, h, c: (b, h, 0, 0)),
                   pl.BlockSpec((1, 1, 1, dqk), lambda b, h, c: (b, h, 0, 0)),
                   pl.BlockSpec((1, 1, 1, 1), lambda b, h, c: (b, h, 0, 0))],
        out_shape=[jax.ShapeDtypeStruct((batch * seq, heads * dv), BF16),
                   jax.ShapeDtypeStruct((batch, heads, dqk, dv), F32),
                   jax.ShapeDtypeStruct((batch, heads, 1, dqk), F32),
                   jax.ShapeDtypeStruct((batch, heads, 1, 1), F32)],
        compiler_params=_params(("parallel", "parallel", "arbitrary"), blocks,
                                8 * _nbytes((chunk, chunk), F32)),
        name="mlstm_prompt",
    )(b_i, b_f, z, z, z, z, z, gates_row, gain.reshape(heads, 1, dv))


def _mlstm_sample_body(bi_ref, bf_ref, qk_ref, v_ref, o_ref, zm_ref, gain_ref, c_ref, n_ref, m_ref,
                       h_ref, co_ref, no_ref, mo_ref, *, heads, dqk, dv, gate_lane, k_scale):
    qk = qk_ref[0]
    q_rows = qk[0:heads, :]
    k_rows = qk[heads:2 * heads, :] * k_scale
    pad = jnp.zeros((dqk - 2 * heads, dqk), F32)
    cols = jnp.concatenate([q_rows, k_rows, pad], axis=0).T
    zm = zm_ref[0]
    ig = _softcap(zm[:, gate_lane:gate_lane + heads] + bi_ref[...])
    fg = _log_sigmoid(_softcap(zm[:, gate_lane + heads:gate_lane + 2 * heads] + bf_ref[...]))
    loga = fg + m_ref[0]
    m_t = jnp.maximum(loga, ig)
    dm = jnp.exp(ig - m_t)
    a = jnp.exp(loga - m_t)
    floor = jnp.exp(-m_t)
    mo_ref[0] = m_t
    v_all = v_ref[0]
    o_all = o_ref[0]
    n_all = n_ref[0]
    for h in range(heads):
        a_h, dm_h = a[:, h:h + 1], dm[:, h:h + 1]
        q_row, k_row = q_rows[h:h + 1, :], k_rows[h:h + 1, :]
        q_col, k_col = cols[:, h:h + 1], cols[:, heads + h:heads + h + 1]
        v_h, n_h = v_all[h:h + 1, :], n_all[h:h + 1, :]
        c_old = c_ref[0, h]
        s = jnp.sum(q_row * k_row, axis=-1, keepdims=True) * dm_h
        num = a_h * jnp.sum(c_old * q_col, axis=0, keepdims=True) + s * v_h
        den = a_h * jnp.sum(q_row * n_h, axis=-1, keepdims=True) + s
        hh = num / jnp.maximum(jnp.abs(den), floor[:, h:h + 1])
        hh = _rms(hh, gain_ref[h:h + 1, :]) * jax.nn.sigmoid(o_all[h:h + 1, :])
        h_ref[0, :, h * dv:(h + 1) * dv] = hh.astype(h_ref.dtype)
        co_ref[0, h] = a_h * c_old + (dm_h * k_col) * v_h
        no_ref[0, h:h + 1, :] = a_h * n_h + dm_h * k_row


def mlstm_sample(qk, v, o, zmisc, b_i, b_f, gain, state_c, state_n, state_m, layer, lay):
    heads, dqk, dv = lay["ml_heads"], lay["ml_dqk"], lay["ml_dv"]
    db = qk.shape[0]
    blocks = [((heads, dqk, dv), F32)] * 2 + [((2 * heads, dqk), F32), ((heads, dv), F32)]
    body = functools.partial(_mlstm_sample_body, heads=heads, dqk=dqk, dv=dv,
                             gate_lane=lay["misc_gate_lane"], k_scale=dqk ** -0.5)
    return pl.pallas_call(
        body,
        grid=(db,),
        in_specs=[pl.BlockSpec((1, heads), lambda b: (0, 0)),
                  pl.BlockSpec((1, heads), lambda b: (0, 0)),
                  pl.BlockSpec((1, 2 * heads, dqk), lambda b: (b, 0, 0)),
                  pl.BlockSpec((1, heads, dv), lambda b: (b, 0, 0)),
                  pl.BlockSpec((1, heads, dv), lambda b: (b, 0, 0)),
                  pl.BlockSpec((1, 1, LANES), lambda b: (b, 0, 0)),
                  pl.BlockSpec((heads, dv), lambda b: (0, 0)),
                  pl.BlockSpec((None, 1, heads, dqk, dv), lambda b: (layer, b, 0, 0, 0)),
                  pl.BlockSpec((None, 1, heads, dqk), lambda b: (layer, b, 0, 0)),
                  pl.BlockSpec((None, 1, 1, heads), lambda b: (layer, b, 0, 0))],
        out_specs=[pl.BlockSpec((1, 1, heads * dv), lambda b: (b, 0, 0)),
                   pl.BlockSpec((1, heads, dqk, dv), lambda b: (b, 0, 0, 0)),
                   pl.BlockSpec((1, heads, dqk), lambda b: (b, 0, 0)),
                   pl.BlockSpec((1, 1, heads), lambda b: (b, 0, 0))],
        out_shape=[jax.ShapeDtypeStruct((db, 1, heads * dv), BF16),
                   jax.ShapeDtypeStruct((db, heads, dqk, dv), F32),
                   jax.ShapeDtypeStruct((db, heads, dqk), F32),
                   jax.ShapeDtypeStruct((db, 1, heads), F32)],
        compiler_params=_params(("parallel",), blocks),
        name="mlstm_sample",
    )(b_i.reshape(1, heads), b_f.reshape(1, heads), qk, v, o, zmisc, gain,
      state_c, state_n, state_m.reshape(state_m.shape[0], db, 1, heads))


def _kv_prep_body(ckv_ref, zm_ref, g_ref, cos_ref, sin_ref, c32_ref, c16_ref, kr32_ref, kr16_ref, *, rope_dim):
    c = _rms(ckv_ref[...], g_ref[...])
    c32_ref[...] = c
    c16_ref[...] = c.astype(BF16)
    x = zm_ref[...]
    kr = (x * cos_ref[...] + _rope_rotate(x) * sin_ref[...])[:, :rope_dim]
    kr32_ref[...] = kr
    kr16_ref[...] = kr.astype(BF16)


def kv_prep(z, g, cos, sin, lay, *, rows, tm, table_blocks):
    kvw, rd = lay["kv_lora"], lay["rope_dim"]
    jc, jm = lay["ckv"] // kvw, lay["misc"] // LANES
    blocks = [((tm, kvw), F32)] * 2 + [((tm, LANES), F32)] * 5
    return pl.pallas_call(
        functools.partial(_kv_prep_body, rope_dim=rd),
        grid=(rows // tm,),
        in_specs=[pl.BlockSpec((tm, kvw), lambda i: (i, jc)),
                  pl.BlockSpec((tm, LANES), lambda i: (i, jm)),
                  pl.BlockSpec((1, kvw), lambda i: (0, 0)),
                  pl.BlockSpec((tm, LANES), lambda i: (i % table_blocks, 0)),
                  pl.BlockSpec((tm, LANES), lambda i: (i % table_blocks, 0))],
        out_specs=[pl.BlockSpec((tm, kvw), lambda i: (i, 0)),
                   pl.BlockSpec((tm, kvw), lambda i: (i, 0)),
                   pl.BlockSpec((tm, rd), lambda i: (i, 0)),
                   pl.BlockSpec((tm, rd), lambda i: (i, 0))],
        out_shape=[jax.ShapeDtypeStruct((rows, kvw), F32), jax.ShapeDtypeStruct((rows, kvw), BF16),
                   jax.ShapeDtypeStruct((rows, rd), F32), jax.ShapeDtypeStruct((rows, rd), BF16)],
        compiler_params=_params(("parallel",), blocks),
        name="mla_kv_prep",
    )(z, z, g.reshape(1, kvw), cos, sin)


def _q_prep_body(cq_ref, g_ref, wuq_ref, wuk_ref, cos_ref, sin_ref, qlat_ref, qrope_ref,
                 *, heads, nope, rope_dim, scale):
    hq = _rms(cq_ref[...], g_ref[...]).astype(BF16)
    q = _dot(hq, wuq_ref[...])
    for h in range(heads):
        qn = q[:, h * nope:(h + 1) * nope].astype(BF16)
        qlat_ref[h] = (_dot(qn, wuk_ref[h]) * scale).astype(qlat_ref.dtype)
    cos = cos_ref[...]
    sin = sin_ref[...]
    base = heads * nope
    per_tile = LANES // rope_dim
    for p in range(heads // per_tile):
        x = q[:, base + p * LANES: base + (p + 1) * LANES]
        r = ((x * cos + _rope_rotate(x) * sin) * scale).astype(qrope_ref.dtype)
        for t in range(per_tile):
            qrope_ref[p * per_tile + t] = r[:, t * rope_dim:(t + 1) * rope_dim]


def q_prep(z, g, wuq, wuk, cos, sin, lay, *, rows, tm, table_blocks):
    heads, nope, rd, kvw, ql = lay["mla_heads"], lay["nope"], lay["rope_dim"], lay["kv_lora"], lay["q_lora"]
    jq = lay["cq"] // ql
    blocks = [((tm, ql), F32), ((ql, heads * (nope + rd)), BF16), ((heads, nope, kvw), BF16),
              ((heads, tm, kvw), BF16), ((heads, tm, LANES), BF16)]
    return pl.pallas_call(
        functools.partial(_q_prep_body, heads=heads, nope=nope, rope_dim=rd, scale=(nope + rd) ** -0.5),
        grid=(rows // tm,),
        in_specs=[pl.BlockSpec((tm, ql), lambda i: (i, jq)),
                  pl.BlockSpec((1, ql), lambda i: (0, 0)),
                  pl.BlockSpec((ql, heads * (nope + rd)), lambda i: (0, 0)),
                  pl.BlockSpec((heads, nope, kvw), lambda i: (0, 0, 0)),
                  pl.BlockSpec((tm, LANES), lambda i: (i % table_blocks, 0)),
                  pl.BlockSpec((tm, LANES), lambda i: (i % table_blocks, 0))],
        out_specs=[pl.BlockSpec((heads, tm, kvw), lambda i: (0, i, 0)),
                   pl.BlockSpec((heads, tm, rd), lambda i: (0, i, 0))],
        out_shape=[jax.ShapeDtypeStruct((heads, rows, kvw), BF16),
                   jax.ShapeDtypeStruct((heads, rows, rd), BF16)],
        compiler_params=_params(("parallel",), blocks, _nbytes((tm, heads * (nope + rd)), F32)),
        name="mla_q_prep",
    )(z, g.reshape(1, ql), wuq, wuk, cos, sin)


def _mla_prompt_body(ql_ref, qr_ref, kc_ref, kr_ref, wuv_ref, o_ref, m_scr, l_scr, acc_scr,
                     *, heads, tq, tk, vdim):
    i = pl.program_id(1)
    kvw = ql_ref.shape[-1]
    q1 = ql_ref[...].reshape(heads * tq, kvw)
    q2 = qr_ref[...].reshape(heads * tq, qr_ref.shape[-1])
    m_scr[...] = jnp.full_like(m_scr, -jnp.inf)
    l_scr[...] = jnp.zeros_like(l_scr)
    acc_scr[...] = jnp.zeros_like(acc_scr)

    def update(start, masked):
        kc = kc_ref[pl.ds(start, tk), :]
        kr = kr_ref[pl.ds(start, tk), :]
        s = _dot_nt(q1, kc) + _dot_nt(q2, kr)
        if masked:
            q_pos = i * tq + lax.broadcasted_iota(jnp.int32, (tq, tk), 0)
            k_pos = start + lax.broadcasted_iota(jnp.int32, (tq, tk), 1)
            s = jnp.where((k_pos <= q_pos)[None], s.reshape(heads, tq, tk), -jnp.inf).reshape(heads * tq, tk)
        m_old = m_scr[...]
        m_new = jnp.maximum(m_old, jnp.max(s, axis=-1, keepdims=True))
        alpha = jnp.exp(m_old - m_new)
        p = jnp.exp(s - m_new)
        l_scr[...] = l_scr[...] * alpha + jnp.sum(p, axis=-1, keepdims=True)
        acc_scr[...] = acc_scr[...] * alpha + _dot(p.astype(BF16), kc)
        m_scr[...] = m_new

    n_full = (i * tq) // tk

    def full_step(j, carry):
        update(pl.multiple_of(j * tk, tk), False)
        return carry

    lax.fori_loop(0, n_full, full_step, 0)
    update(pl.multiple_of(n_full * tk, tk), True)
    o = acc_scr[...] / l_scr[...]
    for h in range(heads):
        oh = o[h * tq:(h + 1) * tq, :].astype(BF16)
        o_ref[:, h * vdim:(h + 1) * vdim] = _dot(oh, wuv_ref[h]).astype(o_ref.dtype)


def mla_prompt(qlat, qrope, kc, kr, wuv, lay, *, batch, seq, tq, tk):
    heads, kvw, rd, vdim = lay["mla_heads"], lay["kv_lora"], lay["rope_dim"], lay["v_dim"]
    nq = seq // tq
    assert tk % tq == 0 and seq % tk == 0
    blocks = [((heads, tq, kvw), BF16), ((heads, tq, LANES), BF16), ((seq, kvw), BF16), ((seq, LANES), BF16),
              ((heads, kvw, vdim), BF16), ((tq, heads * vdim), BF16)]
    scratch = _nbytes((heads * tq, kvw), F32) + 2 * _nbytes((heads * tq, LANES), F32)
    body = functools.partial(_mla_prompt_body, heads=heads, tq=tq, tk=tk, vdim=vdim)
    return pl.pallas_call(
        body,
        grid=(batch, nq),
        in_specs=[pl.BlockSpec((heads, tq, kvw), lambda b, i: (0, b * nq + i, 0)),
                  pl.BlockSpec((heads, tq, rd), lambda b, i: (0, b * nq + i, 0)),
                  pl.BlockSpec((seq, kvw), lambda b, i: (b, 0)),
                  pl.BlockSpec((seq, rd), lambda b, i: (b, 0)),
                  pl.BlockSpec((heads, kvw, vdim), lambda b, i: (0, 0, 0))],
        out_specs=pl.BlockSpec((tq, heads * vdim), lambda b, i: (b * nq + i, 0)),
        out_shape=jax.ShapeDtypeStruct((batch * seq, heads * vdim), BF16),
        scratch_shapes=[pltpu.VMEM((heads * tq, 1), F32), pltpu.VMEM((heads * tq, 1), F32),
                        pltpu.VMEM((heads * tq, kvw), F32)],
        compiler_params=_params(("parallel", "arbitrary"), blocks,
                                scratch + 4 * _nbytes((heads * tq, tk), F32)),
        name="mla_prompt",
    )(qlat, qrope, kc, kr, wuv)


def _mla_sample_body(pt_ref, ql_ref, qr_ref, cn_ref, krn_ref, lat_hbm, kr_hbm, o_ref,
                     lat_buf, kr_buf, sem, m_scr, l_scr, acc_scr, *, layer, pages):
    page = lat_buf.shape[2]
    c = pl.program_id(1)
    n_chunks = pl.num_programs(1)
    step = pl.program_id(0) * n_chunks + c
    total = pl.num_programs(0) * n_chunks
    slot = step % 2

    def chunk_copies(chunk_step, sl):
        copies = []
        for j in range(pages):
            pg = pt_ref[chunk_step * pages + j]
            copies.append(pltpu.make_async_copy(lat_hbm.at[layer, pg], lat_buf.at[sl, j], sem.at[0, sl]))
            copies.append(pltpu.make_async_copy(kr_hbm.at[layer, pg], kr_buf.at[sl, :, pl.ds(j * page, page)],
                                                sem.at[1, sl]))
        return copies

    @pl.when(step == 0)
    def _():
        for cp in chunk_copies(0, 0):
            cp.start()

    @pl.when(step + 1 < total)
    def _():
        for cp in chunk_copies(step + 1, 1 - slot):
            cp.start()

    ql = ql_ref[0]
    qr = qr_ref[0]

    @pl.when(c == 0)
    def _():
        cn = cn_ref[0].astype(F32)
        krn = krn_ref[0].astype(F32)
        m_scr[...] = (jnp.sum(ql.astype(F32) * cn, axis=-1, keepdims=True)
                      + jnp.sum(qr.astype(F32) * krn, axis=-1, keepdims=True))
        l_scr[...] = jnp.ones_like(l_scr)
        acc_scr[...] = jnp.broadcast_to(cn, acc_scr.shape)

    for cp in chunk_copies(step, slot):
        cp.wait()

    keys = lat_buf[slot].reshape(pages * page, lat_buf.shape[-1]).astype(BF16)
    s = _dot_nt(ql, keys) + _dot(qr, kr_buf[slot].astype(BF16))
    m_old = m_scr[...]
    m_new = jnp.maximum(m_old, jnp.max(s, axis=-1, keepdims=True))
    alpha = jnp.exp(m_old - m_new)
    p = jnp.exp(s - m_new)
    l_new = l_scr[...] * alpha + jnp.sum(p, axis=-1, keepdims=True)
    acc = acc_scr[...] * alpha + _dot(p.astype(BF16), keys)
    m_scr[...] = m_new
    l_scr[...] = l_new
    acc_scr[...] = acc

    @pl.when(c == n_chunks - 1)
    def _():
        o_ref[0] = (acc / l_new).astype(o_ref.dtype)


def mla_sample(qlat, qrope, c_new, kr_new, cache_lat, cache_kr_t, page_table, layer, lay, *, pages):
    db, heads, kvw = qlat.shape
    rd = qrope.shape[-1]
    n_pages = page_table.shape[1]
    page = cache_lat.shape[2]
    assert n_pages % pages == 0
    buffers = _nbytes((2, pages, page, kvw), F32) + _nbytes((2, pages, rd, page), F32)
    blocks = [((heads, kvw), BF16), ((heads, LANES), BF16), ((heads, kvw), F32)]
    body = functools.partial(_mla_sample_body, layer=layer, pages=pages)
    hbm = pl.BlockSpec(memory_space=pl.ANY)
    grid_spec = pltpu.PrefetchScalarGridSpec(
        num_scalar_prefetch=1,
        grid=(db, n_pages // pages),
        in_specs=[pl.BlockSpec((1, heads, kvw), lambda b, c, pt: (b, 0, 0)),
                  pl.BlockSpec((1, heads, rd), lambda b, c, pt: (b, 0, 0)),
                  pl.BlockSpec((1, 1, kvw), lambda b, c, pt: (b, 0, 0)),
                  pl.BlockSpec((1, 1, rd), lambda b, c, pt: (b, 0, 0)),
                  hbm, hbm],
        out_specs=pl.BlockSpec((1, heads, kvw), lambda b, c, pt: (b, 0, 0)),
        scratch_shapes=[pltpu.VMEM((2, pages, page, kvw), F32), pltpu.VMEM((2, rd, pages * page), F32),
                        pltpu.SemaphoreType.DMA((2, 2)),
                        pltpu.VMEM((heads, 1), F32), pltpu.VMEM((heads, 1), F32),
                        pltpu.VMEM((heads, kvw), F32)])
    return pl.pallas_call(
        body,
        grid_spec=grid_spec,
        out_shape=jax.ShapeDtypeStruct((db, heads, kvw), BF16),
        compiler_params=_params(("arbitrary", "arbitrary"), blocks, buffers),
        name="mla_sample",
    )(page_table.reshape(-1), qlat, qrope, c_new, kr_new, cache_lat, cache_kr_t)


def _uv_body(o_ref, w_ref, h_ref):
    h_ref[...] = _dot(o_ref[0], w_ref[0]).astype(h_ref.dtype)


def mla_up_v(o_lat, wuv):
    heads, m, kvw = o_lat.shape
    vdim = wuv.shape[-1]
    blocks = [((m, kvw), BF16), ((kvw, vdim), BF16), ((m, vdim), BF16)]
    return pl.pallas_call(
        _uv_body,
        grid=(heads,),
        in_specs=[pl.BlockSpec((1, m, kvw), lambda h: (h, 0, 0)),
                  pl.BlockSpec((1, kvw, vdim), lambda h: (h, 0, 0))],
        out_specs=pl.BlockSpec((m, vdim), lambda h: (0, h)),
        out_shape=jax.ShapeDtypeStruct((m, heads * vdim), BF16),
        compiler_params=_params(("parallel",), blocks),
        name="mla_up_v",
    )(o_lat, wuv)


def _mem_attend_prompt_body(q_ref, k_ref, v_ref, o_ref, *, heads, dim):
    q = q_ref[...]
    k = k_ref[...].astype(BF16)
    v = v_ref[...].astype(BF16)
    scale = dim ** -0.5
    for h in range(heads):
        sl = slice(h * dim, (h + 1) * dim)
        s = _dot_nt(q[:, sl], k[:, sl]) * scale
        p = jnp.exp(s - jnp.max(s, axis=-1, keepdims=True))
        p = p / jnp.sum(p, axis=-1, keepdims=True)
        o_ref[:, sl] = _dot(p.astype(BF16), v[:, sl]).astype(o_ref.dtype)


def mem_attend_prompt(q, kv, lay, *, batch, seq, tq):
    heads, dim, n_mem = lay["mem_heads"], lay["mem_dim"], lay["n_mem"]
    w = heads * dim
    nq = seq // tq
    blocks = [((tq, w), BF16), ((n_mem, w), F32), ((n_mem, w), F32), ((tq, w), BF16)]
    body = functools.partial(_mem_attend_prompt_body, heads=heads, dim=dim)
    return pl.pallas_call(
        body,
        grid=(batch, nq),
        in_specs=[pl.BlockSpec((tq, w), lambda b, i: (b * nq + i, 0)),
                  pl.BlockSpec((n_mem, w), lambda b, i: (b, 0)),
                  pl.BlockSpec((n_mem, w), lambda b, i: (b, 1))],
        out_specs=pl.BlockSpec((tq, w), lambda b, i: (b * nq + i, 0)),
        out_shape=jax.ShapeDtypeStruct((batch * seq, w), BF16),
        compiler_params=_params(("parallel", "parallel"), blocks, 4 * _nbytes((tq, n_mem), F32)),
        name="mem_attend_prompt",
    )(q, kv, kv)


def _mem_attend_sample_body(q_ref, k_ref, v_ref, o_ref, *, group, heads, dim):
    rows = q_ref.shape[1]
    cols = k_ref.shape[1]
    col_head = lax.broadcasted_iota(jnp.int32, (rows, cols), 1) % heads
    row_head = lax.broadcasted_iota(jnp.int32, (rows, cols), 0) % heads
    own_head = col_head == row_head
    scale = dim ** -0.5
    for g in range(group):
        k = k_ref[g].astype(BF16)
        v = v_ref[g].astype(BF16)
        s = jnp.where(own_head, _dot_nt(q_ref[g], k) * scale, -jnp.inf)
        p = jnp.exp(s - jnp.max(s, axis=-1, keepdims=True))
        p = p / jnp.sum(p, axis=-1, keepdims=True)
        o_ref[g] = _dot(p.astype(BF16), v).astype(o_ref.dtype)


def mem_attend_sample(q, cache_k, cache_v, layer, lay, *, group):
    heads, dim = lay["mem_heads"], lay["mem_dim"]
    db, rows, _ = q.shape
    cols = cache_k.shape[2]
    blocks = [((group, cols, dim), F32)] * 2 + [((group, rows, dim), BF16)] * 2
    body = functools.partial(_mem_attend_sample_body, group=group, heads=heads, dim=dim)
    return pl.pallas_call(
        body,
        grid=(db // group,),
        in_specs=[pl.BlockSpec((group, rows, dim), lambda b: (b, 0, 0)),
                  pl.BlockSpec((None, group, cols, dim), lambda b: (layer, b, 0, 0)),
                  pl.BlockSpec((None, group, cols, dim), lambda b: (layer, b, 0, 0))],
        out_specs=pl.BlockSpec((group, rows, dim), lambda b: (b, 0, 0)),
        out_shape=jax.ShapeDtypeStruct((db, rows, dim), BF16),
        compiler_params=_params(("parallel",), blocks),
        name="mem_attend_sample",
    )(q, cache_k, cache_v)


def _layout(shapes):
    lay = dict(shapes)
    qk_w = lay["ml_heads"] * lay["ml_dqk"]
    v_w = lay["ml_heads"] * lay["ml_dv"]
    d = lay["d_model"]
    off = 0
    for name, width in (("q", qk_w), ("k", qk_w), ("v", v_w), ("o", v_w), ("cq", lay["q_lora"]),
                        ("ga", d), ("gb", d), ("ckv", lay["kv_lora"]), ("misc", LANES)):
        lay[name] = off
        off += width
    lay["z_cols"] = off
    lay["misc_gate_lane"] = lay["rope_dim"]
    assert lay["rope_dim"] + 2 * lay["ml_heads"] <= LANES
    return lay


def _repack_w_in(w, lay):
    heads = lay["ml_heads"]
    qk_w = heads * lay["ml_dqk"]
    v_w = heads * lay["ml_dv"]
    d = lay["d_model"]
    o = 0
    src = {}
    for name, width in (("q", qk_w), ("k", qk_w), ("v", v_w), ("i", heads), ("f", heads), ("o", v_w),
                        ("cq", lay["q_lora"]), ("ckv", lay["kv_lora"]), ("kr", lay["rope_dim"]),
                        ("ga", d), ("gb", d)):
        src[name] = (o, o + width)
        o += width
    pad = LANES - lay["rope_dim"] - 2 * heads
    parts = [w[:, src[n][0]:src[n][1]] for n in ("q", "k", "v", "o", "cq", "ga", "gb", "ckv", "kr", "i", "f")]
    parts.append(jnp.zeros((w.shape[0], pad), w.dtype))
    return jnp.concatenate(parts, axis=1).astype(BF16)


def _repack_w_uq(w, lay):
    heads, nope, rd = lay["mla_heads"], lay["nope"], lay["rope_dim"]
    w3 = w.reshape(w.shape[0], heads, nope + rd)
    return jnp.concatenate([w3[:, :, :nope].reshape(w.shape[0], heads * nope),
                            w3[:, :, nope:].reshape(w.shape[0], heads * rd)], axis=1).astype(BF16)


def _rope_tables(pos, rope_dim):
    half = rope_dim // 2
    inv = ROPE_THETA ** (-jnp.arange(half, dtype=F32) / half)
    ang = pos.astype(F32)[:, None] * inv[None, :]
    reps = LANES // half
    return jnp.tile(jnp.cos(ang), (1, reps)), jnp.tile(jnp.sin(ang), (1, reps))


def _tile(n, want):
    if n <= want:
        return n
    t = want - want % BF16_ROWS
    while n % t:
        t -= BF16_ROWS
    return t


def kernel(x_prompt, mem_prompt, x_sample, state_mlstm_C, state_mlstm_n, state_mlstm_m, cache_mla_latent, cache_mla_krope, cache_mem_k, cache_mem_v, page_table, norm_ffn1, ffn1_w_gate, ffn1_w_up, ffn1_w_down, norm_mix, w_in, mlstm_b_i, mlstm_b_f, mlstm_norm, mla_norm_q, mla_w_uq, mla_norm_kv, mla_w_uk, mla_w_uv, w_branch_mlstm, w_branch_mla, w_out, norm_mem, norm_mem_src, mem_w_q, mem_w_k, mem_w_v, mem_w_o, norm_ffn2, ffn2_w_gate, ffn2_w_up, ffn2_w_down, norm_final):
    batch, seq, d_model = x_prompt.shape
    db, dec_seq, _ = x_sample.shape
    assert dec_seq == 1
    depth = norm_ffn1.shape[0]
    n_mem = mem_prompt.shape[1]
    lay = _layout(dict(
        d_model=d_model, ml_heads=state_mlstm_C.shape[2], ml_dqk=state_mlstm_C.shape[3],
        ml_dv=state_mlstm_C.shape[4], mla_heads=mla_w_uk.shape[2], nope=mla_w_uk.shape[3],
        q_lora=mla_w_uq.shape[1], kv_lora=mla_w_uk.shape[1], rope_dim=cache_mla_krope.shape[3],
        v_dim=mla_w_uv.shape[3], mem_heads=cache_mem_k.shape[3], mem_dim=cache_mem_k.shape[4], n_mem=n_mem))
    heads, dqk, dv = lay["ml_heads"], lay["ml_dqk"], lay["ml_dv"]
    mem_heads, mem_dim = lay["mem_heads"], lay["mem_dim"]
    mem_w = mem_heads * mem_dim
    rd, kvw = lay["rope_dim"], lay["kv_lora"]
    mp = batch * seq
    m_all = mp + db

    tm = _tile(m_all, 1024)
    tr = _tile(m_all, 320)
    tn = 512
    chunk = min(256, seq)
    tm_prep = _tile(seq, 256)
    tk_mla = min(512, seq)
    tq_mla = min(128, seq)

    past_len = page_table.shape[1] * cache_mla_latent.shape[2]
    cos_p, sin_p = _rope_tables(jnp.arange(seq), rd)
    cos_s, sin_s = _rope_tables(jnp.full((db,), past_len), rd)

    cache_k4 = cache_mem_k.reshape(depth, db, n_mem * mem_heads, mem_dim)
    cache_v4 = cache_mem_v.reshape(depth, db, n_mem * mem_heads, mem_dim)
    cache_kr_t = jnp.swapaxes(cache_mla_krope, 2, 3)
    mem_src = mem_prompt.reshape(batch * n_mem, d_model)
    ffn1_gate16, ffn1_up16, ffn1_down16 = (w.astype(BF16) for w in (ffn1_w_gate, ffn1_w_up, ffn1_w_down))
    ffn2_gate16, ffn2_up16, ffn2_down16 = (w.astype(BF16) for w in (ffn2_w_gate, ffn2_w_up, ffn2_w_down))

    x = jnp.concatenate([x_prompt.reshape(mp, d_model), x_sample.reshape(db, d_model)], axis=0)
    out_p = [[] for _ in range(7)]
    out_s = [[] for _ in range(5)]
    for l in range(depth):
        w_in_l = _repack_w_in(w_in[l], lay)
        wuq = _repack_w_uq(mla_w_uq[l], lay)
        wuk = jnp.transpose(mla_w_uk[l], (1, 2, 0)).astype(BF16)
        wuv = jnp.transpose(mla_w_uv[l], (1, 0, 2)).astype(BF16)
        w_mem_kv = jnp.concatenate([mem_w_k[l], mem_w_v[l]], axis=1)

        h = rmsnorm(x, norm_ffn1[l], BF16, tm=tr)
        u = swiglu_up(h, ffn1_gate16, ffn1_up16, tm=tm, tn=tn, layer=l)
        x = matmul_residual(u, ffn1_down16, x, 0.5, tm=tm, tn=tn, layer=l)
        h = rmsnorm(x, norm_mix[l], BF16, tm=tr)
        z = matmul(h, w_in_l, F32, tm=tm, tn=tn)
        zs = z[mp:]

        hm_src = rmsnorm(mem_src, norm_mem_src[l], BF16, tm=_tile(batch * n_mem, 256))
        mem_kv = matmul(hm_src, w_mem_kv, F32, tm=_tile(batch * n_mem, 1024), tn=tn)
        gate_cols = z[:mp, lay["misc"] + rd: lay["misc"] + rd + 2 * heads]
        gates_row = gate_cols.reshape(batch, seq, 2, heads).transpose(0, 3, 2, 1)
        hm_p, p_c, p_n, p_m = mlstm_prompt(z, gates_row, mlstm_b_i[l], mlstm_b_f[l], mlstm_norm[l], lay,
                                           batch=batch, seq=seq, chunk=chunk)
        c32, c16, kr32, kr16 = kv_prep(z, mla_norm_kv[l], cos_p, sin_p, lay, rows=mp, tm=tm_prep,
                                       table_blocks=seq // tm_prep)
        qlat, qrope = q_prep(z, mla_norm_q[l], wuq, wuk, cos_p, sin_p, lay, rows=mp, tm=tm_prep,
                             table_blocks=seq // tm_prep)
        ha_p = mla_prompt(qlat, qrope, c16, kr16, wuv, lay, batch=batch, seq=seq, tq=tq_mla, tk=tk_mla)
        for lst, a in zip(out_p, (p_c, p_n.reshape(batch, heads, dqk), p_m.reshape(batch, heads),
                                  c32.reshape(batch, seq, kvw), kr32.reshape(batch, seq, rd),
                                  mem_kv[:, :mem_w].reshape(batch, n_mem, mem_heads, mem_dim),
                                  mem_kv[:, mem_w:].reshape(batch, n_mem, mem_heads, mem_dim))):
            lst.append(a)

        qk_s = zs[:, lay["q"]:lay["v"]].reshape(db, 2 * heads, dqk)
        v_s = zs[:, lay["v"]:lay["o"]].reshape(db, heads, dv)
        o_s = zs[:, lay["o"]:lay["cq"]].reshape(db, heads, dv)
        misc_s = zs[:, lay["misc"]:].reshape(db, 1, LANES)
        hm_s, s_c, s_n, s_m = mlstm_sample(qk_s, v_s, o_s, misc_s, mlstm_b_i[l], mlstm_b_f[l], mlstm_norm[l],
                                           state_mlstm_C, state_mlstm_n, state_mlstm_m, l, lay)
        c32, c16, kr32, kr16 = kv_prep(zs, mla_norm_kv[l], cos_s, sin_s, lay, rows=db, tm=db, table_blocks=1)
        qlat, qrope = q_prep(zs, mla_norm_q[l], wuq, wuk, cos_s, sin_s, lay, rows=db, tm=db, table_blocks=1)
        o_lat = mla_sample(qlat.transpose(1, 0, 2), qrope.transpose(1, 0, 2), c16.reshape(db, 1, kvw),
                           kr16.reshape(db, 1, rd), cache_mla_latent, cache_kr_t, page_table, l, lay,
                           pages=min(64, page_table.shape[1]))
        ha_s = mla_up_v(o_lat.transpose(1, 0, 2), wuv)
        for lst, a in zip(out_s, (s_c, s_n, s_m.reshape(db, heads), c32.reshape(db, 1, kvw), kr32.reshape(db, 1, rd))):
            lst.append(a)

        h_m = jnp.concatenate([hm_p, hm_s.reshape(db, heads * dv)], axis=0)
        h_a = jnp.concatenate([ha_p, ha_s], axis=0)
        merged = gated_merge(h_m, h_a, w_branch_mlstm, w_branch_mla, z, lay["ga"], lay["gb"], tm=tm, tn=tn, layer=l)
        x = matmul_residual(merged, w_out, x, 1.0, tm=tm, tn=tn, layer=l)
        h = rmsnorm(x, norm_mem[l], BF16, tm=tr)
        q = matmul(h, mem_w_q, BF16, tm=tm, tn=tn, layer=l)
        o_p = mem_attend_prompt(q, mem_kv, lay, batch=batch, seq=seq, tq=_tile(seq, 512))
        q_s = jnp.pad(q[mp:].reshape(db, mem_heads, mem_dim), ((0, 0), (0, BF16_ROWS - mem_heads), (0, 0)))
        o_s = mem_attend_sample(q_s, cache_k4, cache_v4, l, lay, group=math.gcd(db, 4))
        o = jnp.concatenate([o_p, o_s[:, :mem_heads].reshape(db, mem_w)], axis=0)
        x = matmul_residual(o, mem_w_o, x, 1.0, tm=tm, tn=tn, layer=l)
        h = rmsnorm(x, norm_ffn2[l], BF16, tm=tr)
        u = swiglu_up(h, ffn2_gate16, ffn2_up16, tm=tm, tn=tn, layer=l)
        x = matmul_residual(u, ffn2_down16, x, 0.5, tm=tm, tn=tn, layer=l)

    y_prompt = rmsnorm(x, norm_final, F32, tm=_tile(mp, 256), rows=mp).reshape(batch, seq, d_model)
    y_sample = rmsnorm(x, norm_final, F32, tm=math.gcd(mp, db), row0=mp, rows=db).reshape(db, 1, d_model)
    return (y_prompt, y_sample) + tuple(jnp.stack(a) for a in out_p) + tuple(jnp.stack(a) for a in out_s)
```
